```python
import jax, jax.numpy as jnp
from jax import lax
import numpy as np

D_MODEL = 1024
BATCH = 2
SEQ = 16384
DEPTH = 4

N_A_LAYERS = DEPTH // 2
N_B_LAYERS = DEPTH - N_A_LAYERS
HEAD_DIM = 64
ATTN_SCALE = HEAD_DIM ** -0.5
ROT_DIM = HEAD_DIM // 4
ROPE_THETA = 500000.0
CONV_CH = 3 * D_MODEL // 4
CONV_WIDTH = 31
MEM_TOKENS = 256
MEM_HEADS = 4
MEM_WIDTH = MEM_HEADS * HEAD_DIM
DIL_GROUPS = ((128, 1), (512, 4), (2048, 16))
N_DIL_GROUPS = len(DIL_GROUPS)
DIL_HEADS = 4
DIL_WIDTH = DIL_HEADS * HEAD_DIM
DIL_BLOCK = 128
A_IN_WIDTH = 2 * CONV_CH + MEM_WIDTH
A_OUT_WIDTH = CONV_CH + MEM_WIDTH
B_IN_WIDTH = N_DIL_GROUPS * DIL_WIDTH + MEM_WIDTH
B_OUT_WIDTH = DIL_WIDTH + MEM_WIDTH
SHARED_KV_WIDTH = N_DIL_GROUPS * 2 * DIL_WIDTH
N_EXPERTS = 32
TOP_K = 4
D_FF = D_MODEL
SWIGLU_ALPHA = 1.702
SWIGLU_LIMIT = 7.0
MOE_BLOCK = 256
DEEPNORM_ALPHA = (2 * DEPTH) ** 0.25
DEEPNORM_BETA = (8 * DEPTH) ** -0.25
LN_EPS = 1e-5

kernel_name = "yoco_conformer_dilated_moe_trunk"


def layer_norm(x, g, b):
    xf = x.astype(jnp.float32)
    mu = jnp.mean(xf, -1, keepdims=True)
    var = jnp.mean(jnp.square(xf - mu), -1, keepdims=True)
    return ((xf - mu) * lax.rsqrt(var + LN_EPS) * g.astype(jnp.float32) + b.astype(jnp.float32)).astype(x.dtype)


def partial_rotary(t, positions):
    half = ROT_DIM // 2
    inv_freq = jnp.power(ROPE_THETA, -jnp.arange(half, dtype=jnp.float32) / half)
    ang = positions.astype(jnp.float32)[..., None] * inv_freq
    cos = jnp.cos(ang)[:, :, None, :]
    sin = jnp.sin(ang)[:, :, None, :]
    tf = t[..., :ROT_DIM].astype(jnp.float32)
    x1, x2 = tf[..., :half], tf[..., half:]
    rot = jnp.concatenate([x1 * cos - x2 * sin, x2 * cos + x1 * sin], -1).astype(t.dtype)
    return jnp.concatenate([rot, t[..., ROT_DIM:]], -1)


def conformer_conv(u, dw, dw_b, g, b):
    a, gate = jnp.split(u, 2, axis=-1)
    h = a * jax.nn.sigmoid(gate)
    h = lax.conv_general_dilated(h, dw[:, None, :], window_strides=(1,), padding=[(CONV_WIDTH - 1, 0)],
                                 dimension_numbers=("NWC", "WIO", "NWC"), feature_group_count=CONV_CH) + dw_b
    return jax.nn.silu(layer_norm(h, g, b))


def memory_attention(q, mem_kv):
    B, S = q.shape[:2]
    k, v = jnp.split(mem_kv, 2, axis=-1)
    k = k.reshape(B, -1, MEM_HEADS, HEAD_DIM)
    v = v.reshape(B, -1, MEM_HEADS, HEAD_DIM)
    q = q.reshape(B, S, MEM_HEADS, HEAD_DIM)
    s = jnp.einsum("bshe,bmhe->bhsm", q, k, preferred_element_type=jnp.float32) * ATTN_SCALE
    p = jax.nn.softmax(s, axis=-1).astype(v.dtype)
    return jnp.einsum("bhsm,bmhe->bshe", p, v).reshape(B, S, MEM_WIDTH)


def dilated_branch(q, k, v, window, dilation):
    B, S, H, E = q.shape
    span = window // dilation
    blk = DIL_BLOCK
    chunk = dilation * blk
    Lp = -(-S // chunk) * chunk
    M = Lp // dilation
    nb = M // blk

    def split(t):
        t = jnp.pad(t, ((0, 0), (0, Lp - S), (0, 0), (0, 0)))
        t = t.reshape(B, M, dilation, H, E).transpose(0, 2, 1, 3, 4)
        return t.reshape(B, dilation, nb, blk, H, E)

    def with_prev(t):
        prev = jnp.pad(t, ((0, 0), (0, 0), (1, 0), (0, 0), (0, 0), (0, 0)))[:, :, :-1]
        return jnp.concatenate([prev, t], axis=3)

    qb = split(q)
    kc = with_prev(split(k))
    vc = with_prev(split(v))
    s = jnp.einsum("brnqhe,brnkhe->brnhqk", qb, kc, preferred_element_type=jnp.float32) * ATTN_SCALE
    qi = jnp.arange(blk)[:, None]
    ki = jnp.arange(2 * blk)[None, :]
    dist = qi + blk - ki
    band = (dist >= 0) & (dist <= span)
    not_before_start = (jnp.arange(nb)[:, None, None] > 0) | (ki >= blk)[None]
    mask = band[None] & not_before_start
    s = jnp.where(mask[None, None, :, None], s, -jnp.inf)
    lse = jax.nn.logsumexp(s, axis=-1)
    p = jnp.exp(s - lse[..., None]).astype(v.dtype)
    o = jnp.einsum("brnhqk,brnkhe->brnqhe", p, vc)
    o = o.reshape(B, dilation, M, H, E).transpose(0, 2, 1, 3, 4).reshape(B, Lp, H, E)[:, :S]
    lse = lse.transpose(0, 1, 2, 4, 3).reshape(B, dilation, M, H).transpose(0, 2, 1, 3).reshape(B, Lp, H)[:, :S]
    return o, lse


def clamped_swiglu(h):
    g, lin = jnp.split(h, 2, axis=-1)
    g = jnp.minimum(g, SWIGLU_LIMIT)
    lin = jnp.clip(lin, -SWIGLU_LIMIT, SWIGLU_LIMIT)
    return g * jax.nn.sigmoid(SWIGLU_ALPHA * g) * (lin + 1.0)


def routed_moe(x, w_r, b_r, w_up, b_up, w_down, b_down):
    B, S, D = x.shape
    T = B * S
    TK = T * TOP_K
    xt = x.reshape(T, D)
    logits = (xt @ w_r + b_r).astype(jnp.float32)
    top_val, top_idx = lax.top_k(logits, TOP_K)
    gates = jax.nn.softmax(top_val, axis=-1)
    e_flat = top_idx.reshape(TK)
    order = jnp.argsort(e_flat)
    e_sorted = e_flat[order]
    tok_sorted = order // TOP_K
    gate_sorted = gates.reshape(TK)[order]
    counts = jnp.bincount(e_flat, length=N_EXPERTS)
    padded = (counts + MOE_BLOCK - 1) // MOE_BLOCK * MOE_BLOCK
    start = jnp.cumsum(counts) - counts
    pend = jnp.cumsum(padded)
    pstart = pend - padded
    dest = pstart[e_sorted] + jnp.arange(TK) - start[e_sorted]
    n_blocks = (TK + N_EXPERTS * (MOE_BLOCK - 1) + MOE_BLOCK - 1) // MOE_BLOCK
    rows = jnp.zeros((n_blocks * MOE_BLOCK, D), x.dtype).at[dest].set(xt[tok_sorted])
    block_expert = jnp.minimum(jnp.searchsorted(pend, jnp.arange(n_blocks) * MOE_BLOCK, side="right"), N_EXPERTS - 1)

    def expert_block(args):
        r, e = args
        h = clamped_swiglu(r @ w_up[e] + b_up[e])
        return h @ w_down[e] + b_down[e]

    y_rows = lax.map(expert_block, (rows.reshape(n_blocks, MOE_BLOCK, D), block_expert))
    y = y_rows.reshape(-1, D)[dest] * gate_sorted[:, None].astype(x.dtype)
    return jax.ops.segment_sum(y, tok_sorted, num_segments=T).reshape(B, S, D)


def setup_inputs(seed: int = 0) -> dict:
    key = jax.random.key(seed)
    ks = jax.random.split(key, 26)
    nrm = lambda k, shape, scale: jax.random.normal(k, shape, jnp.float32) * scale
    D = D_MODEL
    return {
        "x": nrm(ks[0], (BATCH, SEQ, D), 1.0),
        "mem": nrm(ks[1], (BATCH, MEM_TOKENS, D), 1.0),
        "positions": (jnp.arange(SEQ, dtype=jnp.int32)[None, :]
                      + jax.random.randint(ks[2], (BATCH, 1), 0, 4096, dtype=jnp.int32)),
        "a_w_in": nrm(ks[3], (N_A_LAYERS, D, A_IN_WIDTH), D ** -0.5),
        "a_b_in": nrm(ks[4], (N_A_LAYERS, A_IN_WIDTH), 0.02),
        "a_dw": nrm(ks[5], (N_A_LAYERS, CONV_WIDTH, CONV_CH), CONV_WIDTH ** -0.5),
        "a_dw_b": nrm(ks[6], (N_A_LAYERS, CONV_CH), 0.02),
        "a_cn_g": 1.0 + nrm(ks[7], (N_A_LAYERS, CONV_CH), 0.02),
        "a_cn_b": nrm(ks[8], (N_A_LAYERS, CONV_CH), 0.02),
        "a_w_out": nrm(ks[9], (N_A_LAYERS, A_OUT_WIDTH, D), A_OUT_WIDTH ** -0.5 * DEEPNORM_BETA),
        "a_b_out": nrm(ks[10], (N_A_LAYERS, D), 0.02),
        "w_kv_shared": nrm(ks[11], (D, SHARED_KV_WIDTH), D ** -0.5),
        "b_w_q": nrm(ks[12], (N_B_LAYERS, D, B_IN_WIDTH), D ** -0.5),
        "b_w_out": nrm(ks[13], (N_B_LAYERS, B_OUT_WIDTH, D), B_OUT_WIDTH ** -0.5 * DEEPNORM_BETA),
        "b_b_out": nrm(ks[14], (N_B_LAYERS, D), 0.02),
        "mem_w_kv": nrm(ks[15], (DEPTH, D, 2 * MEM_WIDTH), D ** -0.5),
        "ln_g": 1.0 + nrm(ks[16], (DEPTH, 2, D), 0.02),
        "ln_b": nrm(ks[17], (DEPTH, 2, D), 0.02),
        "router_w": nrm(ks[18], (DEPTH, D, N_EXPERTS), D ** -0.5),
        "router_b": nrm(ks[19], (DEPTH, N_EXPERTS), 0.01),
        "exp_w_up": nrm(ks[20], (DEPTH, N_EXPERTS, D, 2 * D_FF), D ** -0.5),
        "exp_b_up": nrm(ks[21], (DEPTH, N_EXPERTS, 2 * D_FF), 0.02),
        "exp_w_down": nrm(ks[22], (DEPTH, N_EXPERTS, D_FF, D), D_FF ** -0.5 * DEEPNORM_BETA),
        "exp_b_down": nrm(ks[23], (DEPTH, N_EXPERTS, D), 0.02),
    }


def reference(x, mem, positions, a_w_in, a_b_in, a_dw, a_dw_b, a_cn_g, a_cn_b, a_w_out, a_b_out,
              w_kv_shared, b_w_q, b_w_out, b_b_out, mem_w_kv, ln_g, ln_b,
              router_w, router_b, exp_w_up, exp_b_up, exp_w_down, exp_b_down):
    B, S, D = x.shape
    shared_k = None
    shared_v = None
    for l in range(DEPTH):
        mem_kv = mem @ mem_w_kv[l]
        if l < N_A_LAYERS:
            h = x @ a_w_in[l] + a_b_in[l]
            conv_out = conformer_conv(h[..., :2 * CONV_CH], a_dw[l], a_dw_b[l], a_cn_g[l], a_cn_b[l])
            mem_out = memory_attention(h[..., 2 * CONV_CH:], mem_kv)
            y = jnp.concatenate([conv_out, mem_out], axis=-1) @ a_w_out[l] + a_b_out[l]
        else:
            j = l - N_A_LAYERS
            if j == 0:
                kv = (x @ w_kv_shared).reshape(B, S, N_DIL_GROUPS, 2, DIL_HEADS, HEAD_DIM)
                shared_k = partial_rotary(kv[:, :, :, 0].reshape(B, S, N_DIL_GROUPS * DIL_HEADS, HEAD_DIM),
                                          positions).reshape(B, S, N_DIL_GROUPS, DIL_HEADS, HEAD_DIM)
                shared_v = kv[:, :, :, 1]
            h = x @ b_w_q[j]
            q_dil = partial_rotary(h[..., :N_DIL_GROUPS * DIL_WIDTH].reshape(B, S, N_DIL_GROUPS * DIL_HEADS, HEAD_DIM),
                                   positions).reshape(B, S, N_DIL_GROUPS, DIL_HEADS, HEAD_DIM)
            outs, lses = [], []
            for g, (window, dilation) in enumerate(DIL_GROUPS):
                o_g, lse_g = dilated_branch(q_dil[:, :, g], shared_k[:, :, g], shared_v[:, :, g], window, dilation)
                outs.append(o_g)
                lses.append(lse_g)
            wts = jax.nn.softmax(jnp.stack(lses), axis=0)
            dil_out = jnp.einsum("gbsh,gbshe->bshe", wts, jnp.stack(outs).astype(jnp.float32))
            dil_out = dil_out.astype(x.dtype).reshape(B, S, DIL_WIDTH)
            mem_out = memory_attention(h[..., N_DIL_GROUPS * DIL_WIDTH:], mem_kv)
            y = jnp.concatenate([dil_out, mem_out], axis=-1) @ b_w_out[j] + b_b_out[j]
        x = layer_norm(DEEPNORM_ALPHA * x + y, ln_g[l, 0], ln_b[l, 0])
        f = routed_moe(x, router_w[l], router_b[l], exp_w_up[l], exp_b_up[l], exp_w_down[l], exp_b_down[l])
        x = layer_norm(DEEPNORM_ALPHA * x + f, ln_g[l, 1], ln_b[l, 1])
    return x
```

```python
import functools

import jax
import jax.numpy as jnp
from jax import lax
from jax.experimental import pallas as pl
from jax.experimental.pallas import tpu as pltpu

D_MODEL = 1024
DEPTH = 4
N_A_LAYERS = 2
HEAD_DIM = 64
ROT_DIM = 16
ROPE_THETA = 500000.0
CONV_CH = 768
CONV_WIDTH = 31
MEM_TOKENS = 256
MEM_WIDTH = 256
DIL_GROUPS = ((128, 1), (512, 4), (2048, 16))
DIL_WIDTH = 256
DIL_BLOCK = 128
N_EXPERTS = 32
TOP_K = 4
D_FF = 1024
SWIGLU_ALPHA = 1.702
SWIGLU_LIMIT = 7.0
DEEPNORM_ALPHA = (2 * DEPTH) ** 0.25
LN_EPS = 1e-5
ATTN_SCALE = HEAD_DIM ** -0.5

LANES = 128
TOKEN_TILE = 512
DIL_TILE = 2048
EXPERT_ROWS = 256
CONV_ROWS = 32
VMEM_LIMIT = 56 * 1024 * 1024

F32 = jnp.float32
BF16 = jnp.bfloat16
NEG = -1e30


def _layer_norm(v, g, b):
    mu = jnp.mean(v, axis=-1, keepdims=True)
    c = v - mu
    var = jnp.mean(c * c, axis=-1, keepdims=True)
    return c * lax.rsqrt(var + LN_EPS) * g + b


def _nt_dot(a, b, **kw):
    return lax.dot_general(a, b, (((1,), (1,)), ((), ())), preferred_element_type=F32, **kw)


def _mem_attention(qm_bf16, kst_ref, vst_ref):
    out = None
    for h in range(4):
        s = _nt_dot(qm_bf16, kst_ref[h])
        m = jnp.max(s, axis=-1, keepdims=True)
        p = jnp.exp(s - m)
        l = jnp.sum(p, axis=-1, keepdims=True)
        pn = (p * (1.0 / l)).astype(BF16)
        u = jnp.dot(pn, vst_ref[h], preferred_element_type=F32)
        out = u if out is None else out + u
    return out


def _residual_ln_router(x_res, y, lng_ref, lnb_ref, wrt_ref, br_ref,
                        x1_ref, x1b_ref, idx_ref, gate_ref, rank_ref, cnt_ref):
    ts = x_res.shape[0]
    x1 = _layer_norm(DEEPNORM_ALPHA * x_res + y, lng_ref[...], lnb_ref[...])
    x1_ref[...] = x1
    x1b_ref[...] = x1.astype(BF16)
    logit = _nt_dot(wrt_ref[...], x1, precision=lax.Precision.HIGHEST) + br_ref[...]
    row = lax.broadcasted_iota(jnp.int32, (N_EXPERTS, ts), 0)
    vals, onehots = [], []
    for k in range(TOP_K):
        m = jnp.max(logit, axis=0, keepdims=True)
        idx = jnp.min(jnp.where(logit == m, row, N_EXPERTS), axis=0, keepdims=True)
        oh = row == idx
        logit = jnp.where(oh, -jnp.inf, logit)
        idx_ref[k:k + 1, :] = idx
        vals.append(m)
        onehots.append(oh)
    exps = [jnp.exp(v - vals[0]) for v in vals]
    inv = 1.0 / (exps[0] + exps[1] + exps[2] + exps[3])
    for k in range(TOP_K):
        gate_ref[k:k + 1, :] = exps[k] * inv
    oh_all = (onehots[0] | onehots[1] | onehots[2] | onehots[3])
    oh_f = jnp.where(oh_all, 1.0, 0.0)
    ri = lax.broadcasted_iota(jnp.int32, (ts, ts), 0)
    ci = lax.broadcasted_iota(jnp.int32, (ts, ts), 1)
    tri = jnp.where(ri < ci, 1.0, 0.0).astype(BF16)
    cum = jnp.dot(oh_f.astype(BF16), tri, preferred_element_type=F32)
    for k in range(TOP_K):
        rk = jnp.sum(jnp.where(onehots[k], cum, 0.0), axis=0, keepdims=True)
        rank_ref[k:k + 1, :] = rk.astype(jnp.int32)
    cnt = jnp.sum(oh_f, axis=1, keepdims=True)
    cnt_ref[...] = jnp.broadcast_to(cnt, (N_EXPERTS, LANES))


def _mem_kv_kernel(mem_ref, w_ref, kst_ref, vst_ref):
    kv = jnp.dot(mem_ref[...], w_ref[...], preferred_element_type=F32)
    k = kv[:, :MEM_WIDTH] * ATTN_SCALE
    v = kv[:, MEM_WIDTH:]
    head = lax.broadcasted_iota(jnp.int32, (MEM_TOKENS, MEM_WIDTH), 1) // HEAD_DIM
    for h in range(4):
        kst_ref[h] = jnp.where(head == h, k, 0.0).astype(BF16)
        vst_ref[h] = jnp.where(head == h, v, 0.0).astype(BF16)


def _mem_kv(mem_b, w_b):
    B = mem_b.shape[0]
    out = jax.ShapeDtypeStruct((DEPTH, B, 4, MEM_TOKENS, MEM_WIDTH), BF16)
    return pl.pallas_call(
        _mem_kv_kernel,
        grid=(DEPTH, B),
        in_specs=[pl.BlockSpec((None, MEM_TOKENS, D_MODEL), lambda l, b: (b, 0, 0)),
                  pl.BlockSpec((None, D_MODEL, 2 * MEM_WIDTH), lambda l, b: (l, 0, 0))],
        out_specs=[pl.BlockSpec((None, None, 4, MEM_TOKENS, MEM_WIDTH), lambda l, b: (l, b, 0, 0, 0))] * 2,
        out_shape=[out, out],
        name="mem_kv",
    )(mem_b, w_b)


def _layer_a_kernel(tiles_per_batch,
                    x_ref, win_ref, bin_ref, dw_ref, dwb_ref, cng_ref, cnb_ref, kst_ref, vst_ref,
                    wout_ref, bout_ref, lng_ref, lnb_ref, wrt_ref, br_ref,
                    x1_ref, x1b_ref, idx_ref, gate_ref, rank_ref, cnt_ref,
                    hpad_ref, shift_ref, cat_ref):
    ts = x_ref.shape[0]
    i = pl.program_id(0)
    x = x_ref[...]
    h = jnp.dot(x.astype(BF16), win_ref[...], preferred_element_type=F32) + bin_ref[...]
    hg = h[:, :CONV_CH] * jax.nn.sigmoid(h[:, CONV_CH:2 * CONV_CH])
    qm = h[:, 2 * CONV_CH:].astype(BF16)

    @pl.when(i % tiles_per_batch == 0)
    def _():
        hpad_ref[0:32, :] = jnp.zeros((32, CONV_CH), F32)

    hpad_ref[32:32 + ts, :] = hg
    for b in range(8):
        n = ts + 8 * ((CONV_WIDTH - 1 - b) // 8)
        shift_ref[b, 0:n, :] = hpad_ref[pl.ds(2 + b, n), :]
    hpad_ref[0:32, :] = hpad_ref[ts:ts + 32, :]

    def conv_rows(c, carry):
        r0 = pl.multiple_of(c * CONV_ROWS, CONV_ROWS)
        acc = jnp.zeros((CONV_ROWS, CONV_CH), F32) + dwb_ref[...]
        for k in range(CONV_WIDTH):
            a, b = divmod(k, 8)
            acc = acc + shift_ref[b, pl.ds(r0 + 8 * a, CONV_ROWS), :] * dw_ref[k:k + 1, :]
        cn = _layer_norm(acc, cng_ref[...], cnb_ref[...])
        cat_ref[pl.ds(r0, CONV_ROWS), 0:CONV_CH] = (cn * jax.nn.sigmoid(cn)).astype(BF16)
        return carry

    lax.fori_loop(0, ts // CONV_ROWS, conv_rows, 0)

    cat_ref[:, CONV_CH:] = _mem_attention(qm, kst_ref, vst_ref).astype(BF16)
    y = jnp.dot(cat_ref[...], wout_ref[...], preferred_element_type=F32) + bout_ref[...]
    _residual_ln_router(x, y, lng_ref, lnb_ref, wrt_ref, br_ref,
                        x1_ref, x1b_ref, idx_ref, gate_ref, rank_ref, cnt_ref)


def _row(v):
    return v.reshape(1, -1).astype(F32)


def _const_spec(shape):
    nd = len(shape)
    return pl.BlockSpec(shape, lambda i: (0,) * nd)


def _router_out(T, ts):
    nt = T // ts
    shapes = [jax.ShapeDtypeStruct((T, D_MODEL), F32), jax.ShapeDtypeStruct((T, D_MODEL), BF16),
              jax.ShapeDtypeStruct((TOP_K, T), jnp.int32), jax.ShapeDtypeStruct((TOP_K, T), F32),
              jax.ShapeDtypeStruct((TOP_K, T), jnp.int32), jax.ShapeDtypeStruct((nt * N_EXPERTS, LANES), F32)]
    specs = [pl.BlockSpec((ts, D_MODEL), lambda i: (i, 0)), pl.BlockSpec((ts, D_MODEL), lambda i: (i, 0)),
             pl.BlockSpec((TOP_K, ts), lambda i: (0, i)), pl.BlockSpec((TOP_K, ts), lambda i: (0, i)),
             pl.BlockSpec((TOP_K, ts), lambda i: (0, i)), pl.BlockSpec((N_EXPERTS, LANES), lambda i: (i, 0))]
    return shapes, specs


def _layer_a(x, S, win, bin_, dw, dwb, cng, cnb, kst, vst, wout, bout, lng, lnb, wrt, br):
    T = x.shape[0]
    ts = TOKEN_TILE
    tpb = S // ts
    out_shapes, out_specs = _router_out(T, ts)
    kv_spec = pl.BlockSpec((None, 4, MEM_TOKENS, MEM_WIDTH), lambda i: (i // tpb, 0, 0, 0))
    a_in = 2 * CONV_CH + MEM_WIDTH
    return pl.pallas_call(
        functools.partial(_layer_a_kernel, tpb),
        grid=(T // ts,),
        in_specs=[pl.BlockSpec((ts, D_MODEL), lambda i: (i, 0)),
                  _const_spec((D_MODEL, a_in)), _const_spec((1, a_in)),
                  _const_spec((32, CONV_CH)), _const_spec((1, CONV_CH)),
                  _const_spec((1, CONV_CH)), _const_spec((1, CONV_CH)),
                  kv_spec, kv_spec,
                  _const_spec((D_MODEL, D_MODEL)), _const_spec((1, D_MODEL)),
                  _const_spec((1, D_MODEL)), _const_spec((1, D_MODEL)),
                  _const_spec((N_EXPERTS, D_MODEL)), _const_spec((N_EXPERTS, 1))],
        out_specs=out_specs,
        out_shape=out_shapes,
        scratch_shapes=[pltpu.VMEM((ts + 32, CONV_CH), F32),
                        pltpu.VMEM((8, ts + 24, CONV_CH), F32),
                        pltpu.VMEM((ts, D_MODEL), BF16)],
        compiler_params=pltpu.CompilerParams(dimension_semantics=("arbitrary",),
                                             vmem_limit_bytes=VMEM_LIMIT),
        name="layer_a",
    )(x, win, bin_, dw, dwb, cng, cnb, kst, vst, wout, bout, lng, lnb, wrt, br)


def _rotary(v, cos_t, sin_t, low):
    partner = jnp.where(low, pltpu.roll(v, LANES - 8, axis=1), pltpu.roll(v, 8, axis=1))
    return v * cos_t + partner * sin_t


def _rope_tables(pos_ref, freq_ref):
    e = lax.broadcasted_iota(jnp.int32, (1, LANES), 1) % HEAD_DIM
    ang = pos_ref[...].astype(F32) * freq_ref[...]
    low = e < ROT_DIM // 2
    rot = e < ROT_DIM
    cos_t = jnp.where(rot, jnp.cos(ang), 1.0)
    sn = jnp.sin(ang)
    sin_t = jnp.where(low, -sn, jnp.where(rot, sn, 0.0))
    return cos_t, sin_t, low


def _layer_b_proj_kernel(with_kv, x_ref, pos_ref, freq_ref, wq_ref, *rest):
    if with_kv:
        wkv_ref, q_ref, qm_ref, k_ref, v_ref = rest
    else:
        q_ref, qm_ref = rest
    xb = x_ref[...].astype(BF16)
    cos_t, sin_t, low = _rope_tables(pos_ref, freq_ref)
    q = jnp.dot(xb, wq_ref[...], preferred_element_type=F32)
    for g in range(3):
        for s in range(2):
            c0 = g * DIL_WIDTH + s * LANES
            q_ref[g, s] = _rotary(q[:, c0:c0 + LANES], cos_t, sin_t, low) * ATTN_SCALE
    qm_ref[...] = q[:, 3 * DIL_WIDTH:].astype(BF16)
    if with_kv:
        kv = jnp.dot(xb, wkv_ref[...], preferred_element_type=F32)
        for g in range(3):
            for s in range(2):
                c0 = g * 2 * DIL_WIDTH + s * LANES
                k_ref[g, s] = _rotary(kv[:, c0:c0 + LANES], cos_t, sin_t, low)
                v_ref[g, s] = kv[:, c0 + DIL_WIDTH:c0 + DIL_WIDTH + LANES]


def _layer_b_proj(x, B, S, pos, freq, wq, wkv):
    T = x.shape[0]
    ts = TOKEN_TILE
    tpb = S // ts
    with_kv = wkv is not None
    slab = jax.ShapeDtypeStruct((3, B, 2, S, LANES), F32)
    slab_spec = pl.BlockSpec((3, None, 2, ts, LANES), lambda i: (0, i // tpb, 0, i % tpb, 0))
    in_specs = [pl.BlockSpec((ts, D_MODEL), lambda i: (i, 0)),
                pl.BlockSpec((ts, 1), lambda i: (i, 0)),
                _const_spec((1, LANES)),
                _const_spec((D_MODEL, D_MODEL))]
    args = [x, pos, freq, wq]
    out_shape = [slab, jax.ShapeDtypeStruct((T, MEM_WIDTH), BF16)]
    out_specs = [slab_spec, pl.BlockSpec((ts, MEM_WIDTH), lambda i: (i, 0))]
    if with_kv:
        in_specs.append(_const_spec((D_MODEL, 6 * DIL_WIDTH)))
        args.append(wkv)
        out_shape += [slab, slab]
        out_specs += [slab_spec, slab_spec]
    return pl.pallas_call(
        functools.partial(_layer_b_proj_kernel, with_kv),
        grid=(T // ts,),
        in_specs=in_specs, out_specs=out_specs, out_shape=out_shape,
        compiler_params=pltpu.CompilerParams(dimension_semantics=("arbitrary",),
                                             vmem_limit_bytes=VMEM_LIMIT),
        name="layer_b_proj_kv" if with_kv else "layer_b_proj",
    )(*args)


def _dilated_kernel(d, q_ref, k_ref, v_ref, o_ref, lse_ref, kbuf_ref, vbuf_ref):
    i = pl.program_id(1)
    tile = DIL_TILE
    blk = DIL_BLOCK

    @pl.when(i == 0)
    def _():
        kbuf_ref[:, 0:tile, :] = jnp.zeros((2, tile, LANES), F32)
        vbuf_ref[:, 0:tile, :] = jnp.zeros((2, tile, LANES), F32)

    kbuf_ref[:, tile:, :] = k_ref[...]
    vbuf_ref[:, tile:, :] = v_ref[...]

    def rows(start, n):
        return pl.ds(start, n) if d == 1 else pl.ds(start, n, stride=d)

    qi = lax.broadcasted_iota(jnp.int32, (blk, 2 * blk), 0)
    ki = lax.broadcasted_iota(jnp.int32, (blk, 2 * blk), 1)
    dist = qi + blk - ki
    band = (dist >= 0) & (dist <= blk)
    head = lax.broadcasted_iota(jnp.int32, (1, LANES), 1) // HEAD_DIM

    def block(j, carry):
        c = j // d
        r = j % d
        qs = c * (blk * d) + r
        ks = tile + qs - blk * d
        first = jnp.logical_and(i == 0, c == 0)
        valid = band & jnp.logical_or(jnp.logical_not(first), ki >= blk)
        for s in range(2):
            q = q_ref[s, rows(qs, blk), :].astype(BF16)
            kk = kbuf_ref[s, rows(ks, 2 * blk), :]
            vv = vbuf_ref[s, rows(ks, 2 * blk), :]
            o_acc = jnp.zeros((blk, LANES), F32)
            lse_acc = jnp.zeros((blk, LANES), F32)
            for h in range(2):
                hm = head == h
                sc = _nt_dot(q, jnp.where(hm, kk, 0.0).astype(BF16))
                sc = jnp.where(valid, sc, NEG)
                m = jnp.max(sc, axis=-1, keepdims=True)
                p = jnp.exp(sc - m)
                l = jnp.sum(p, axis=-1, keepdims=True)
                u = jnp.dot(p.astype(BF16), jnp.where(hm, vv, 0.0).astype(BF16),
                            preferred_element_type=F32)
                o_acc = o_acc + u * (1.0 / l)
                lse_acc = jnp.where(hm, m + jnp.log(l), lse_acc)
            o_ref[s, rows(qs, blk), :] = o_acc
            lse_ref[s, rows(qs, blk), :] = lse_acc
        return carry

    lax.fori_loop(0, tile // blk, block, 0)
    kbuf_ref[:, 0:tile, :] = kbuf_ref[:, tile:, :]
    vbuf_ref[:, 0:tile, :] = vbuf_ref[:, tile:, :]


def _dilated_group(g, d, q, k, v):
    _, B, _, S, _ = q.shape
    tile = DIL_TILE
    in_spec = pl.BlockSpec((None, None, 2, tile, LANES), lambda b, i: (g, b, 0, i, 0))
    out_spec = pl.BlockSpec((None, 2, tile, LANES), lambda b, i: (b, 0, i, 0))
    out = jax.ShapeDtypeStruct((B, 2, S, LANES), F32)
    return pl.pallas_call(
        functools.partial(_dilated_kernel, d),
        grid=(B, S // tile),
        in_specs=[in_spec, in_spec, in_spec],
        out_specs=[out_spec, out_spec],
        out_shape=[out, out],
        scratch_shapes=[pltpu.VMEM((2, 2 * tile, LANES), F32), pltpu.VMEM((2, 2 * tile, LANES), F32)],
        compiler_params=pltpu.CompilerParams(dimension_semantics=("arbitrary", "arbitrary"),
                                             vmem_limit_bytes=VMEM_LIMIT),
        name=f"dilated_d{d}",
    )(q, k, v)


def _layer_b_out_kernel(x_ref, o0_ref, l0_ref, o1_ref, l1_ref, o2_ref, l2_ref, qm_ref, kst_ref, vst_ref,
                        wout_ref, bout_ref, lng_ref, lnb_ref, wrt_ref, br_ref,
                        x1_ref, x1b_ref, idx_ref, gate_ref, rank_ref, cnt_ref, cat_ref):
    for s in range(2):
        lses = [l0_ref[s], l1_ref[s], l2_ref[s]]
        outs = [o0_ref[s], o1_ref[s], o2_ref[s]]
        m = jnp.maximum(jnp.maximum(lses[0], lses[1]), lses[2])
        es = [jnp.exp(l - m) for l in lses]
        inv = 1.0 / (es[0] + es[1] + es[2])
        dil = (es[0] * inv) * outs[0] + (es[1] * inv) * outs[1] + (es[2] * inv) * outs[2]
        cat_ref[:, s * LANES:(s + 1) * LANES] = dil.astype(BF16)
    cat_ref[:, DIL_WIDTH:] = _mem_attention(qm_ref[...], kst_ref, vst_ref).astype(BF16)
    y = jnp.dot(cat_ref[...], wout_ref[...], preferred_element_type=F32) + bout_ref[...]
    _residual_ln_router(x_ref[...], y, lng_ref, lnb_ref, wrt_ref, br_ref,
                        x1_ref, x1b_ref, idx_ref, gate_ref, rank_ref, cnt_ref)


def _layer_b_out(x, S, dil, qm, kst, vst, wout, bout, lng, lnb, wrt, br):
    T = x.shape[0]
    ts = TOKEN_TILE
    tpb = S // ts
    out_shapes, out_specs = _router_out(T, ts)
    kv_spec = pl.BlockSpec((None, 4, MEM_TOKENS, MEM_WIDTH), lambda i: (i // tpb, 0, 0, 0))
    slab_spec = pl.BlockSpec((None, 2, ts, LANES), lambda i: (i // tpb, 0, i % tpb, 0))
    b_out = DIL_WIDTH + MEM_WIDTH
    return pl.pallas_call(
        _layer_b_out_kernel,
        grid=(T // ts,),
        in_specs=[pl.BlockSpec((ts, D_MODEL), lambda i: (i, 0))] + [slab_spec] * 6 + [
            pl.BlockSpec((ts, MEM_WIDTH), lambda i: (i, 0)), kv_spec, kv_spec,
            _const_spec((b_out, D_MODEL)), _const_spec((1, D_MODEL)),
            _const_spec((1, D_MODEL)), _const_spec((1, D_MODEL)),
            _const_spec((N_EXPERTS, D_MODEL)), _const_spec((N_EXPERTS, 1))],
        out_specs=out_specs,
        out_shape=out_shapes,
        scratch_shapes=[pltpu.VMEM((ts, b_out), BF16)],
        compiler_params=pltpu.CompilerParams(dimension_semantics=("arbitrary",),
                                             vmem_limit_bytes=VMEM_LIMIT),
        name="layer_b_out",
    )(x, *dil, qm, kst, vst, wout, bout, lng, lnb, wrt, br)


def _expert_kernel(be_ref, rows_ref, gate_ref, wup_ref, bup_ref, wdn_ref, bdn_ref, y_ref):
    h = jnp.dot(rows_ref[...], wup_ref[...], preferred_element_type=F32) + bup_ref[...]
    g = jnp.minimum(h[:, :D_FF], SWIGLU_LIMIT)
    lin = jnp.clip(h[:, D_FF:], -SWIGLU_LIMIT, SWIGLU_LIMIT)
    act = g * jax.nn.sigmoid(SWIGLU_ALPHA * g) * (lin + 1.0)
    y = jnp.dot(act.astype(BF16), wdn_ref[...], preferred_element_type=F32) + bdn_ref[...]
    y_ref[...] = y * gate_ref[...]


def _experts(block_expert, rows, row_gate, wup, bup, wdn, bdn):
    nr = rows.shape[0]
    r = EXPERT_ROWS
    grid_spec = pltpu.PrefetchScalarGridSpec(
        num_scalar_prefetch=1,
        grid=(nr // r,),
        in_specs=[pl.BlockSpec((r, D_MODEL), lambda i, be: (i, 0)),
                  pl.BlockSpec((r, 1), lambda i, be: (i, 0)),
                  pl.BlockSpec((None, D_MODEL, 2 * D_FF), lambda i, be: (be[i], 0, 0)),
                  pl.BlockSpec((None, 1, 2 * D_FF), lambda i, be: (be[i], 0, 0)),
                  pl.BlockSpec((None, D_FF, D_MODEL), lambda i, be: (be[i], 0, 0)),
                  pl.BlockSpec((None, 1, D_MODEL), lambda i, be: (be[i], 0, 0))],
        out_specs=pl.BlockSpec((r, D_MODEL), lambda i, be: (i, 0)),
    )
    return pl.pallas_call(
        _expert_kernel,
        grid_spec=grid_spec,
        out_shape=jax.ShapeDtypeStruct((nr, D_MODEL), F32),
        compiler_params=pltpu.CompilerParams(dimension_semantics=("arbitrary",),
                                             vmem_limit_bytes=VMEM_LIMIT),
        name="experts",
    )(block_expert, rows, row_gate, wup, bup, wdn, bdn)


def _post_ffn_kernel(x_ref, f_ref, g_ref, b_ref, o_ref):
    o_ref[...] = _layer_norm(DEEPNORM_ALPHA * x_ref[...] + f_ref[...], g_ref[...], b_ref[...])


def _post_ffn(x1, f, g, b):
    T = x1.shape[0]
    ts = TOKEN_TILE
    spec = pl.BlockSpec((ts, D_MODEL), lambda i: (i, 0))
    return pl.pallas_call(
        _post_ffn_kernel,
        grid=(T // ts,),
        in_specs=[spec, spec, _const_spec((1, D_MODEL)), _const_spec((1, D_MODEL))],
        out_specs=spec,
        out_shape=jax.ShapeDtypeStruct((T, D_MODEL), F32),
        name="post_ffn",
    )(x1, f, g, b)


def _moe(x1, x1b, idx_t, gate_t, rank_t, cnt, wup, bup, wdn, bdn, lng, lnb):
    T = x1.shape[0]
    ts = TOKEN_TILE
    nt = T // ts
    r = EXPERT_ROWS
    tk = T * TOP_K
    counts_tile = cnt[:, 0].reshape(nt, N_EXPERTS).astype(jnp.int32)
    counts = jnp.sum(counts_tile, axis=0)
    tile_off = jnp.cumsum(counts_tile, axis=0) - counts_tile
    padded = (counts + r - 1) // r * r
    pend = jnp.cumsum(padded)
    pstart = pend - padded
    n_blocks = (tk + N_EXPERTS * (r - 1) + r - 1) // r
    block_expert = jnp.minimum(
        jnp.searchsorted(pend, jnp.arange(n_blocks, dtype=jnp.int32) * r, side="right"),
        N_EXPERTS - 1).astype(jnp.int32)
    idx = idx_t.T
    tile_of = jnp.arange(T, dtype=jnp.int32)[:, None] // ts
    dest = pstart[idx] + tile_off[tile_of, idx] + rank_t.T
    dflat = dest.reshape(tk)
    tok = jnp.arange(tk, dtype=jnp.int32) // TOP_K
    src = jnp.zeros((n_blocks * r,), jnp.int32).at[dflat].set(tok)
    row_gate = jnp.zeros((n_blocks * r,), F32).at[dflat].set(gate_t.T.reshape(tk))
    rows = x1b[src]
    y = _experts(block_expert, rows, row_gate.reshape(-1, 1), wup, bup, wdn, bdn)
    f = jnp.sum(y[dest], axis=1)
    return _post_ffn(x1, f, lng, lnb)


def kernel(x, mem, positions, a_w_in, a_b_in, a_dw, a_dw_b, a_cn_g, a_cn_b, a_w_out, a_b_out, w_kv_shared, b_w_q, b_w_out, b_b_out, mem_w_kv, ln_g, ln_b, router_w, router_b, exp_w_up, exp_b_up, exp_w_down, exp_b_down):
    B, S, D = x.shape
    T = B * S
    assert D == D_MODEL and S % DIL_TILE == 0 and S % TOKEN_TILE == 0
    xt = x.reshape(T, D)
    kst, vst = _mem_kv(mem.astype(BF16), mem_w_kv.astype(BF16))
    pos = positions.reshape(T, 1).astype(jnp.int32)
    half = ROT_DIM // 2
    inv_freq = jnp.power(ROPE_THETA, -jnp.arange(half, dtype=F32) / half)
    e = jnp.arange(LANES) % HEAD_DIM
    freq = jnp.where(e < ROT_DIM, inv_freq[e % half], 0.0).reshape(1, LANES).astype(F32)
    dw_pad = jnp.pad(a_dw, ((0, 0), (0, 32 - CONV_WIDTH), (0, 0)))
    shared_kv = None
    for l in range(DEPTH):
        wrt = router_w[l].T
        br = router_b[l].reshape(N_EXPERTS, 1)
        lng0, lnb0 = _row(ln_g[l, 0]), _row(ln_b[l, 0])
        if l < N_A_LAYERS:
            outs = _layer_a(xt, S, a_w_in[l].astype(BF16), _row(a_b_in[l]), dw_pad[l], _row(a_dw_b[l]),
                            _row(a_cn_g[l]), _row(a_cn_b[l]), kst[l], vst[l],
                            a_w_out[l].astype(BF16), _row(a_b_out[l]), lng0, lnb0, wrt, br)
        else:
            j = l - N_A_LAYERS
            if j == 0:
                q, qm, k, v = _layer_b_proj(xt, B, S, pos, freq, b_w_q[j].astype(BF16), w_kv_shared.astype(BF16))
                shared_kv = (k, v)
            else:
                q, qm = _layer_b_proj(xt, B, S, pos, freq, b_w_q[j].astype(BF16), None)
            dil = []
            for g, (window, dilation) in enumerate(DIL_GROUPS):
                assert window // dilation == DIL_BLOCK and DIL_TILE % (DIL_BLOCK * dilation) == 0
                dil += _dilated_group(g, dilation, q, shared_kv[0], shared_kv[1])
            outs = _layer_b_out(xt, S, dil, qm, kst[l], vst[l], b_w_out[j].astype(BF16), _row(b_b_out[j]),
                                lng0, lnb0, wrt, br)
        x1, x1b, idx_t, gate_t, rank_t, cnt = outs
        xt = _moe(x1, x1b, idx_t, gate_t, rank_t, cnt,
                  exp_w_up[l].astype(BF16), exp_b_up[l].reshape(N_EXPERTS, 1, 2 * D_FF),
                  exp_w_down[l].astype(BF16), exp_b_down[l].reshape(N_EXPERTS, 1, D_MODEL),
                  _row(ln_g[l, 1]), _row(ln_b[l, 1]))
    return xt.reshape(B, S, D)
```

```python
import functools

import jax
import jax.numpy as jnp
from jax import lax
from jax.experimental import pallas as pl
from jax.experimental.pallas import tpu as pltpu

D_MODEL = 1024
DEPTH = 4
N_A_LAYERS = 2
HEAD_DIM = 64
ROT_DIM = 16
ROPE_THETA = 500000.0
CONV_CH = 768
CONV_WIDTH = 31
MEM_TOKENS = 256
MEM_WIDTH = 256
DIL_GROUPS = ((128, 1), (512, 4), (2048, 16))
DIL_WIDTH = 256
DIL_BLOCK = 128
N_EXPERTS = 32
TOP_K = 4
D_FF = 1024
SWIGLU_ALPHA = 1.702
SWIGLU_LIMIT = 7.0
DEEPNORM_ALPHA = (2 * DEPTH) ** 0.25
LN_EPS = 1e-5
ATTN_SCALE = HEAD_DIM ** -0.5

LANES = 128
TOKEN_TILE = 512
DIL_TILE = 2048
EXPERT_ROWS = 256
CONV_ROWS = 32
SEG_ALIGN = 16
SEG_BITS = tuple(SEG_ALIGN << b for b in reversed(range(6)))
SORT_ROWS = 256
SORTED_ROWS = 2560
VMEM_LIMIT = 56 * 1024 * 1024

F32 = jnp.float32
BF16 = jnp.bfloat16
NEG = -1e30


def _layer_norm(v, g, b):
    mu = jnp.mean(v, axis=-1, keepdims=True)
    c = v - mu
    var = jnp.mean(c * c, axis=-1, keepdims=True)
    return c * lax.rsqrt(var + LN_EPS) * g + b


def _nt_dot(a, b, **kw):
    return lax.dot_general(a, b, (((1,), (1,)), ((), ())), preferred_element_type=F32, **kw)


def _mem_attention(qm_bf16, kst_ref, vst_ref):
    out = None
    for h in range(4):
        s = _nt_dot(qm_bf16, kst_ref[h])
        m = jnp.max(s, axis=-1, keepdims=True)
        p = jnp.exp(s - m)
        l = jnp.sum(p, axis=-1, keepdims=True)
        pn = (p * (1.0 / l)).astype(BF16)
        u = jnp.dot(pn, vst_ref[h], preferred_element_type=F32)
        out = u if out is None else out + u
    return out


def _residual_ln_router(x_res, y, lng_ref, lnb_ref, wrt_ref, br_ref,
                        x1_ref, x1b_ref, gate_ref, pos_ref, cnt_ref):
    ts = x_res.shape[0]
    x1 = _layer_norm(DEEPNORM_ALPHA * x_res + y, lng_ref[...], lnb_ref[...])
    x1_ref[...] = x1
    x1b_ref[...] = x1.astype(BF16)
    logit = _nt_dot(wrt_ref[...], x1, precision=lax.Precision.HIGHEST) + br_ref[...]
    row = lax.broadcasted_iota(jnp.int32, (N_EXPERTS, ts), 0)
    vals, onehots = [], []
    for k in range(TOP_K):
        m = jnp.max(logit, axis=0, keepdims=True)
        idx = jnp.min(jnp.where(logit == m, row, N_EXPERTS), axis=0, keepdims=True)
        oh = row == idx
        logit = jnp.where(oh, -jnp.inf, logit)
        vals.append(m)
        onehots.append(oh)
    exps = [jnp.exp(v - vals[0]) for v in vals]
    inv = 1.0 / (exps[0] + exps[1] + exps[2] + exps[3])
    for k in range(TOP_K):
        gate_ref[k:k + 1, :] = exps[k] * inv
    oh_all = (onehots[0] | onehots[1] | onehots[2] | onehots[3])
    oh_f = jnp.where(oh_all, 1.0, 0.0)
    ri = lax.broadcasted_iota(jnp.int32, (ts, ts), 0)
    ci = lax.broadcasted_iota(jnp.int32, (ts, ts), 1)
    tri = jnp.where(ri < ci, 1.0, 0.0).astype(BF16)
    cum = jnp.dot(oh_f.astype(BF16), tri, preferred_element_type=F32)
    cnt = jnp.sum(oh_f, axis=1, keepdims=True)
    cnt_b = jnp.broadcast_to(cnt, (N_EXPERTS, LANES))
    cnt_ref[...] = cnt_b
    units = jnp.floor((cnt_b + (SEG_ALIGN - 1)) * (1.0 / SEG_ALIGN))
    er = lax.broadcasted_iota(jnp.int32, (N_EXPERTS, N_EXPERTS), 0)
    ec = lax.broadcasted_iota(jnp.int32, (N_EXPERTS, N_EXPERTS), 1)
    low_tri = jnp.where(ec < er, 1.0, 0.0).astype(BF16)
    seg_start = jnp.dot(low_tri, units.astype(BF16), preferred_element_type=F32)[:, 0:1] * SEG_ALIGN
    for k in range(TOP_K):
        rk = jnp.sum(jnp.where(onehots[k], cum + seg_start, 0.0), axis=0, keepdims=True)
        pos_ref[k:k + 1, :] = rk.astype(jnp.int32)


def _mem_kv_kernel(mem_ref, w_ref, kst_ref, vst_ref):
    kv = jnp.dot(mem_ref[...], w_ref[...], preferred_element_type=F32)
    k = kv[:, :MEM_WIDTH] * ATTN_SCALE
    v = kv[:, MEM_WIDTH:]
    head = lax.broadcasted_iota(jnp.int32, (MEM_TOKENS, MEM_WIDTH), 1) // HEAD_DIM
    for h in range(4):
        kst_ref[h] = jnp.where(head == h, k, 0.0).astype(BF16)
        vst_ref[h] = jnp.where(head == h, v, 0.0).astype(BF16)


def _mem_kv(mem_b, w_b):
    B = mem_b.shape[0]
    out = jax.ShapeDtypeStruct((DEPTH, B, 4, MEM_TOKENS, MEM_WIDTH), BF16)
    return pl.pallas_call(
        _mem_kv_kernel,
        grid=(DEPTH, B),
        in_specs=[pl.BlockSpec((None, MEM_TOKENS, D_MODEL), lambda l, b: (b, 0, 0)),
                  pl.BlockSpec((None, D_MODEL, 2 * MEM_WIDTH), lambda l, b: (l, 0, 0))],
        out_specs=[pl.BlockSpec((None, None, 4, MEM_TOKENS, MEM_WIDTH), lambda l, b: (l, b, 0, 0, 0))] * 2,
        out_shape=[out, out],
        name="mem_kv",
    )(mem_b, w_b)


def _layer_a_kernel(tiles_per_batch,
                    x_ref, win_ref, bin_ref, dw_ref, dwb_ref, cng_ref, cnb_ref, kst_ref, vst_ref,
                    wout_ref, bout_ref, lng_ref, lnb_ref, wrt_ref, br_ref,
                    x1_ref, x1b_ref, gate_ref, pos_ref, cnt_ref,
                    hpad_ref, shift_ref, cat_ref):
    ts = x_ref.shape[0]
    i = pl.program_id(0)
    x = x_ref[...]
    h = jnp.dot(x.astype(BF16), win_ref[...], preferred_element_type=F32) + bin_ref[...]
    hg = h[:, :CONV_CH] * jax.nn.sigmoid(h[:, CONV_CH:2 * CONV_CH])
    qm = h[:, 2 * CONV_CH:].astype(BF16)

    @pl.when(i % tiles_per_batch == 0)
    def _():
        hpad_ref[0:32, :] = jnp.zeros((32, CONV_CH), F32)

    hpad_ref[32:32 + ts, :] = hg
    for b in range(8):
        n = ts + 8 * ((CONV_WIDTH - 1 - b) // 8)
        shift_ref[b, 0:n, :] = hpad_ref[pl.ds(2 + b, n), :]
    hpad_ref[0:32, :] = hpad_ref[ts:ts + 32, :]

    def conv_rows(c, carry):
        r0 = pl.multiple_of(c * CONV_ROWS, CONV_ROWS)
        acc = jnp.zeros((CONV_ROWS, CONV_CH), F32) + dwb_ref[...]
        for k in range(CONV_WIDTH):
            a, b = divmod(k, 8)
            acc = acc + shift_ref[b, pl.ds(r0 + 8 * a, CONV_ROWS), :] * dw_ref[k:k + 1, :]
        cn = _layer_norm(acc, cng_ref[...], cnb_ref[...])
        cat_ref[pl.ds(r0, CONV_ROWS), 0:CONV_CH] = (cn * jax.nn.sigmoid(cn)).astype(BF16)
        return carry

    lax.fori_loop(0, ts // CONV_ROWS, conv_rows, 0)

    cat_ref[:, CONV_CH:] = _mem_attention(qm, kst_ref, vst_ref).astype(BF16)
    y = jnp.dot(cat_ref[...], wout_ref[...], preferred_element_type=F32) + bout_ref[...]
    _residual_ln_router(x, y, lng_ref, lnb_ref, wrt_ref, br_ref,
                        x1_ref, x1b_ref, gate_ref, pos_ref, cnt_ref)


def _row(v):
    return v.reshape(1, -1).astype(F32)


def _const_spec(shape):
    nd = len(shape)
    return pl.BlockSpec(shape, lambda i: (0,) * nd)


def _router_out(T, ts):
    nt = T // ts
    shapes = [jax.ShapeDtypeStruct((T, D_MODEL), F32), jax.ShapeDtypeStruct((T, D_MODEL), BF16),
              jax.ShapeDtypeStruct((TOP_K, T), F32), jax.ShapeDtypeStruct((TOP_K, T), jnp.int32),
              jax.ShapeDtypeStruct((nt * N_EXPERTS, LANES), F32)]
    specs = [pl.BlockSpec((ts, D_MODEL), lambda i: (i, 0)), pl.BlockSpec((ts, D_MODEL), lambda i: (i, 0)),
             pl.BlockSpec((TOP_K, ts), lambda i: (0, i)), pl.BlockSpec((TOP_K, ts), lambda i: (0, i)),
             pl.BlockSpec((N_EXPERTS, LANES), lambda i: (i, 0))]
    return shapes, specs


def _layer_a(x, S, win, bin_, dw, dwb, cng, cnb, kst, vst, wout, bout, lng, lnb, wrt, br):
    T = x.shape[0]
    ts = TOKEN_TILE
    tpb = S // ts
    out_shapes, out_specs = _router_out(T, ts)
    kv_spec = pl.BlockSpec((None, 4, MEM_TOKENS, MEM_WIDTH), lambda i: (i // tpb, 0, 0, 0))
    a_in = 2 * CONV_CH + MEM_WIDTH
    return pl.pallas_call(
        functools.partial(_layer_a_kernel, tpb),
        grid=(T // ts,),
        in_specs=[pl.BlockSpec((ts, D_MODEL), lambda i: (i, 0)),
                  _const_spec((D_MODEL, a_in)), _const_spec((1, a_in)),
                  _const_spec((32, CONV_CH)), _const_spec((1, CONV_CH)),
                  _const_spec((1, CONV_CH)), _const_spec((1, CONV_CH)),
                  kv_spec, kv_spec,
                  _const_spec((D_MODEL, D_MODEL)), _const_spec((1, D_MODEL)),
                  _const_spec((1, D_MODEL)), _const_spec((1, D_MODEL)),
                  _const_spec((N_EXPERTS, D_MODEL)), _const_spec((N_EXPERTS, 1))],
        out_specs=out_specs,
        out_shape=out_shapes,
        scratch_shapes=[pltpu.VMEM((ts + 32, CONV_CH), F32),
                        pltpu.VMEM((8, ts + 24, CONV_CH), F32),
                        pltpu.VMEM((ts, D_MODEL), BF16)],
        compiler_params=pltpu.CompilerParams(dimension_semantics=("arbitrary",),
                                             vmem_limit_bytes=VMEM_LIMIT),
        name="layer_a",
    )(x, win, bin_, dw, dwb, cng, cnb, kst, vst, wout, bout, lng, lnb, wrt, br)


def _rotary(v, cos_t, sin_t, low):
    partner = jnp.where(low, pltpu.roll(v, LANES - 8, axis=1), pltpu.roll(v, 8, axis=1))
    return v * cos_t + partner * sin_t


def _rope_tables(pos_ref, freq_ref):
    e = lax.broadcasted_iota(jnp.int32, (1, LANES), 1) % HEAD_DIM
    ang = pos_ref[...].astype(F32) * freq_ref[...]
    low = e < ROT_DIM // 2
    rot = e < ROT_DIM
    cos_t = jnp.where(rot, jnp.cos(ang), 1.0)
    sn = jnp.sin(ang)
    sin_t = jnp.where(low, -sn, jnp.where(rot, sn, 0.0))
    return cos_t, sin_t, low


def _layer_b_proj_kernel(with_kv, x_ref, pos_ref, freq_ref, wq_ref, *rest):
    if with_kv:
        wkv_ref, q_ref, qm_ref, k_ref, v_ref = rest
    else:
        q_ref, qm_ref = rest
    xb = x_ref[...].astype(BF16)
    cos_t, sin_t, low = _rope_tables(pos_ref, freq_ref)
    q = jnp.dot(xb, wq_ref[...], preferred_element_type=F32)
    for g in range(3):
        for s in range(2):
            c0 = g * DIL_WIDTH + s * LANES
            q_ref[g, s] = _rotary(q[:, c0:c0 + LANES], cos_t, sin_t, low) * ATTN_SCALE
    qm_ref[...] = q[:, 3 * DIL_WIDTH:].astype(BF16)
    if with_kv:
        kv = jnp.dot(xb, wkv_ref[...], preferred_element_type=F32)
        for g in range(3):
            for s in range(2):
                c0 = g * 2 * DIL_WIDTH + s * LANES
                k_ref[g, s] = _rotary(kv[:, c0:c0 + LANES], cos_t, sin_t, low)
                v_ref[g, s] = kv[:, c0 + DIL_WIDTH:c0 + DIL_WIDTH + LANES]


def _layer_b_proj(x, B, S, pos, freq, wq, wkv):
    T = x.shape[0]
    ts = TOKEN_TILE
    tpb = S // ts
    with_kv = wkv is not None
    slab = jax.ShapeDtypeStruct((3, B, 2, S, LANES), F32)
    slab_spec = pl.BlockSpec((3, None, 2, ts, LANES), lambda i: (0, i // tpb, 0, i % tpb, 0))
    in_specs = [pl.BlockSpec((ts, D_MODEL), lambda i: (i, 0)),
                pl.BlockSpec((ts, 1), lambda i: (i, 0)),
                _const_spec((1, LANES)),
                _const_spec((D_MODEL, D_MODEL))]
    args = [x, pos, freq, wq]
    out_shape = [slab, jax.ShapeDtypeStruct((T, MEM_WIDTH), BF16)]
    out_specs = [slab_spec, pl.BlockSpec((ts, MEM_WIDTH), lambda i: (i, 0))]
    if with_kv:
        in_specs.append(_const_spec((D_MODEL, 6 * DIL_WIDTH)))
        args.append(wkv)
        out_shape += [slab, slab]
        out_specs += [slab_spec, slab_spec]
    return pl.pallas_call(
        functools.partial(_layer_b_proj_kernel, with_kv),
        grid=(T // ts,),
        in_specs=in_specs, out_specs=out_specs, out_shape=out_shape,
        compiler_params=pltpu.CompilerParams(dimension_semantics=("arbitrary",),
                                             vmem_limit_bytes=VMEM_LIMIT),
        name="layer_b_proj_kv" if with_kv else "layer_b_proj",
    )(*args)


def _dilated_kernel(d, q_ref, k_ref, v_ref, o_ref, lse_ref, kbuf_ref, vbuf_ref):
    i = pl.program_id(1)
    tile = DIL_TILE
    blk = DIL_BLOCK

    @pl.when(i == 0)
    def _():
        kbuf_ref[:, 0:tile, :] = jnp.zeros((2, tile, LANES), F32)
        vbuf_ref[:, 0:tile, :] = jnp.zeros((2, tile, LANES), F32)

    kbuf_ref[:, tile:, :] = k_ref[...]
    vbuf_ref[:, tile:, :] = v_ref[...]

    def rows(start, n):
        return pl.ds(start, n) if d == 1 else pl.ds(start, n, stride=d)

    qi = lax.broadcasted_iota(jnp.int32, (blk, 2 * blk), 0)
    ki = lax.broadcasted_iota(jnp.int32, (blk, 2 * blk), 1)
    dist = qi + blk - ki
    band = (dist >= 0) & (dist <= blk)
    head = lax.broadcasted_iota(jnp.int32, (1, LANES), 1) // HEAD_DIM

    def block(j, carry):
        c = j // d
        r = j % d
        qs = c * (blk * d) + r
        ks = tile + qs - blk * d
        first = jnp.logical_and(i == 0, c == 0)
        valid = band & jnp.logical_or(jnp.logical_not(first), ki >= blk)
        for s in range(2):
            q = q_ref[s, rows(qs, blk), :].astype(BF16)
            kk = kbuf_ref[s, rows(ks, 2 * blk), :]
            vv = vbuf_ref[s, rows(ks, 2 * blk), :]
            o_acc = jnp.zeros((blk, LANES), F32)
            lse_acc = jnp.zeros((blk, LANES), F32)
            for h in range(2):
                hm = head == h
                sc = _nt_dot(q, jnp.where(hm, kk, 0.0).astype(BF16))
                sc = jnp.where(valid, sc, NEG)
                m = jnp.max(sc, axis=-1, keepdims=True)
                p = jnp.exp(sc - m)
                l = jnp.sum(p, axis=-1, keepdims=True)
                u = jnp.dot(p.astype(BF16), jnp.where(hm, vv, 0.0).astype(BF16),
                            preferred_element_type=F32)
                o_acc = o_acc + u * (1.0 / l)
                lse_acc = jnp.where(hm, m + jnp.log(l), lse_acc)
            o_ref[s, rows(qs, blk), :] = o_acc
            lse_ref[s, rows(qs, blk), :] = lse_acc
        return carry

    lax.fori_loop(0, tile // blk, block, 0)
    kbuf_ref[:, 0:tile, :] = kbuf_ref[:, tile:, :]
    vbuf_ref[:, 0:tile, :] = vbuf_ref[:, tile:, :]


def _dilated_group(g, d, q, k, v):
    _, B, _, S, _ = q.shape
    tile = DIL_TILE
    in_spec = pl.BlockSpec((None, None, 2, tile, LANES), lambda b, i: (g, b, 0, i, 0))
    out_spec = pl.BlockSpec((None, 2, tile, LANES), lambda b, i: (b, 0, i, 0))
    out = jax.ShapeDtypeStruct((B, 2, S, LANES), F32)
    return pl.pallas_call(
        functools.partial(_dilated_kernel, d),
        grid=(B, S // tile),
        in_specs=[in_spec, in_spec, in_spec],
        out_specs=[out_spec, out_spec],
        out_shape=[out, out],
        scratch_shapes=[pltpu.VMEM((2, 2 * tile, LANES), F32), pltpu.VMEM((2, 2 * tile, LANES), F32)],
        compiler_params=pltpu.CompilerParams(dimension_semantics=("arbitrary", "arbitrary"),
                                             vmem_limit_bytes=VMEM_LIMIT),
        name=f"dilated_d{d}",
    )(q, k, v)


def _layer_b_out_kernel(x_ref, o0_ref, l0_ref, o1_ref, l1_ref, o2_ref, l2_ref, qm_ref, kst_ref, vst_ref,
                        wout_ref, bout_ref, lng_ref, lnb_ref, wrt_ref, br_ref,
                        x1_ref, x1b_ref, gate_ref, pos_ref, cnt_ref, cat_ref):
    for s in range(2):
        lses = [l0_ref[s], l1_ref[s], l2_ref[s]]
        outs = [o0_ref[s], o1_ref[s], o2_ref[s]]
        m = jnp.maximum(jnp.maximum(lses[0], lses[1]), lses[2])
        es = [jnp.exp(l - m) for l in lses]
        inv = 1.0 / (es[0] + es[1] + es[2])
        dil = (es[0] * inv) * outs[0] + (es[1] * inv) * outs[1] + (es[2] * inv) * outs[2]
        cat_ref[:, s * LANES:(s + 1) * LANES] = dil.astype(BF16)
    cat_ref[:, DIL_WIDTH:] = _mem_attention(qm_ref[...], kst_ref, vst_ref).astype(BF16)
    y = jnp.dot(cat_ref[...], wout_ref[...], preferred_element_type=F32) + bout_ref[...]
    _residual_ln_router(x_ref[...], y, lng_ref, lnb_ref, wrt_ref, br_ref,
                        x1_ref, x1b_ref, gate_ref, pos_ref, cnt_ref)


def _layer_b_out(x, S, dil, qm, kst, vst, wout, bout, lng, lnb, wrt, br):
    T = x.shape[0]
    ts = TOKEN_TILE
    tpb = S // ts
    out_shapes, out_specs = _router_out(T, ts)
    kv_spec = pl.BlockSpec((None, 4, MEM_TOKENS, MEM_WIDTH), lambda i: (i // tpb, 0, 0, 0))
    slab_spec = pl.BlockSpec((None, 2, ts, LANES), lambda i: (i // tpb, 0, i % tpb, 0))
    b_out = DIL_WIDTH + MEM_WIDTH
    return pl.pallas_call(
        _layer_b_out_kernel,
        grid=(T // ts,),
        in_specs=[pl.BlockSpec((ts, D_MODEL), lambda i: (i, 0))] + [slab_spec] * 6 + [
            pl.BlockSpec((ts, MEM_WIDTH), lambda i: (i, 0)), kv_spec, kv_spec,
            _const_spec((b_out, D_MODEL)), _const_spec((1, D_MODEL)),
            _const_spec((1, D_MODEL)), _const_spec((1, D_MODEL)),
            _const_spec((N_EXPERTS, D_MODEL)), _const_spec((N_EXPERTS, 1))],
        out_specs=out_specs,
        out_shape=out_shapes,
        scratch_shapes=[pltpu.VMEM((ts, b_out), BF16)],
        compiler_params=pltpu.CompilerParams(dimension_semantics=("arbitrary",),
                                             vmem_limit_bytes=VMEM_LIMIT),
        name="layer_b_out",
    )(x, *dil, qm, kst, vst, wout, bout, lng, lnb, wrt, br)


def _expert_kernel(be_ref, nused_ref, rows_ref, gate_ref, wup_ref, bup_ref, wdn_ref, bdn_ref, y_ref):
    used = pl.program_id(0) < nused_ref[0]

    @pl.when(jnp.logical_not(used))
    def _():
        y_ref[...] = jnp.zeros(y_ref.shape, F32)

    @pl.when(used)
    def _():
        h = jnp.dot(rows_ref[...], wup_ref[...], preferred_element_type=F32) + bup_ref[...]
        g = jnp.minimum(h[:, :D_FF], SWIGLU_LIMIT)
        lin = jnp.clip(h[:, D_FF:], -SWIGLU_LIMIT, SWIGLU_LIMIT)
        act = g * jax.nn.sigmoid(SWIGLU_ALPHA * g) * (lin + 1.0)
        y = jnp.dot(act.astype(BF16), wdn_ref[...], preferred_element_type=F32) + bdn_ref[...]
        gate = gate_ref[...]
        for c in range(D_MODEL // LANES):
            y_ref[:, c * LANES:(c + 1) * LANES] = y[:, c * LANES:(c + 1) * LANES] * gate


def _experts(block_expert, n_used, rows, row_gate, wup, bup, wdn, bdn):
    nr = rows.shape[0]
    r = EXPERT_ROWS

    def blk(i, be, nu):
        return jnp.minimum(i, nu[0] - 1)

    grid_spec = pltpu.PrefetchScalarGridSpec(
        num_scalar_prefetch=2,
        grid=(nr // r,),
        in_specs=[pl.BlockSpec((r, D_MODEL), lambda i, be, nu: (blk(i, be, nu), 0)),
                  pl.BlockSpec((r, LANES), lambda i, be, nu: (blk(i, be, nu), 0)),
                  pl.BlockSpec((None, D_MODEL, 2 * D_FF), lambda i, be, nu: (be[blk(i, be, nu)], 0, 0)),
                  pl.BlockSpec((None, 1, 2 * D_FF), lambda i, be, nu: (be[blk(i, be, nu)], 0, 0)),
                  pl.BlockSpec((None, D_FF, D_MODEL), lambda i, be, nu: (be[blk(i, be, nu)], 0, 0)),
                  pl.BlockSpec((None, 1, D_MODEL), lambda i, be, nu: (be[blk(i, be, nu)], 0, 0))],
        out_specs=pl.BlockSpec((r, D_MODEL), lambda i, be, nu: (i, 0)),
    )
    return pl.pallas_call(
        _expert_kernel,
        grid_spec=grid_spec,
        out_shape=jax.ShapeDtypeStruct((nr, D_MODEL), F32),
        compiler_params=pltpu.CompilerParams(dimension_semantics=("arbitrary",),
                                             vmem_limit_bytes=VMEM_LIMIT),
        name="experts",
    )(block_expert, n_used, rows, row_gate, wup, bup, wdn, bdn)


def _segment_copies(tile, len_ref, loc_ref, glob_ref, make_copies, act):
    def per_expert(e, carry):
        n = len_ref[tile * N_EXPERTS + e]
        loc = loc_ref[tile * N_EXPERTS + e]
        glob = glob_ref[tile * N_EXPERTS + e]
        off = jnp.int32(0)
        for bit in SEG_BITS:
            part = n & bit

            @pl.when(part != 0)
            def _():
                lo = pl.multiple_of(loc + off, SEG_ALIGN)
                go = pl.multiple_of(glob + off, SEG_ALIGN)
                for cp in make_copies(lo, go, bit):
                    act(cp)

            off = off + part
        return carry

    lax.fori_loop(0, N_EXPERTS, per_expert, 0)


def _dispatch_kernel(len_ref, loc_ref, glob_ref, tot_ref,
                     xb_ref, pos_ref, gate_ref, rows_init, gates_init,
                     rows_hbm, gates_hbm, xs_ref, gs_ref, sem):
    del rows_init, gates_init
    i = pl.program_id(0)
    nt = pl.num_programs(0)
    slot = i % 2
    ts = xb_ref.shape[0]

    def copies(s):
        def make(lo, go, n):
            return (pltpu.make_async_copy(xs_ref.at[s, pl.ds(lo, n)], rows_hbm.at[pl.ds(go, n)], sem.at[0, s]),
                    pltpu.make_async_copy(gs_ref.at[s, pl.ds(lo, n)], gates_hbm.at[pl.ds(go, n)], sem.at[1, s]))
        return make

    tables = (len_ref, loc_ref, glob_ref)

    @pl.when(i >= 2)
    def _():
        _segment_copies(i - 2, *tables, copies(slot), lambda cp: cp.wait())

    xb = xb_ref[...]

    def sort_rows(c, carry):
        r0 = pl.multiple_of(c * SORT_ROWS, SORT_ROWS)
        prow = lax.broadcasted_iota(jnp.int32, (SORT_ROWS, ts), 0) + r0
        hit = None
        wg = jnp.zeros((SORT_ROWS, ts), F32)
        for k in range(TOP_K):
            hk = prow == pos_ref[k:k + 1, :]
            hit = hk if hit is None else (hit | hk)
            wg = jnp.where(hk, gate_ref[k:k + 1, :], wg)
        onehot = jnp.where(hit, 1.0, 0.0).astype(BF16)
        xs_ref[slot, pl.ds(r0, SORT_ROWS), :] = jnp.dot(onehot, xb, preferred_element_type=F32).astype(BF16)
        g = jnp.sum(wg, axis=1, keepdims=True)
        gs_ref[slot, pl.ds(r0, SORT_ROWS), :] = jnp.broadcast_to(g, (SORT_ROWS, LANES))
        return carry

    lax.fori_loop(0, (tot_ref[i] + SORT_ROWS - 1) // SORT_ROWS, sort_rows, 0)
    _segment_copies(i, *tables, copies(slot), lambda cp: cp.start())

    @pl.when(i == nt - 1)
    def _():
        @pl.when(i >= 1)
        def _():
            _segment_copies(i - 1, *tables, copies(1 - slot), lambda cp: cp.wait())
        _segment_copies(i, *tables, copies(slot), lambda cp: cp.wait())


def _dispatch(seg_len, seg_loc, seg_glob, tile_tot, x1b, pos_t, gate_t, n_rows):
    T = x1b.shape[0]
    ts = TOKEN_TILE
    any_spec = pl.BlockSpec(memory_space=pl.ANY)
    grid_spec = pltpu.PrefetchScalarGridSpec(
        num_scalar_prefetch=4,
        grid=(T // ts,),
        in_specs=[pl.BlockSpec((ts, D_MODEL), lambda i, *_: (i, 0)),
                  pl.BlockSpec((TOP_K, ts), lambda i, *_: (0, i)),
                  pl.BlockSpec((TOP_K, ts), lambda i, *_: (0, i)),
                  any_spec, any_spec],
        out_specs=[any_spec, any_spec],
        scratch_shapes=[pltpu.VMEM((2, SORTED_ROWS, D_MODEL), BF16),
                        pltpu.VMEM((2, SORTED_ROWS, LANES), F32),
                        pltpu.SemaphoreType.DMA((2, 2))],
    )
    rows0 = jnp.zeros((n_rows, D_MODEL), BF16)
    gates0 = jnp.zeros((n_rows, LANES), F32)
    return pl.pallas_call(
        _dispatch_kernel,
        grid_spec=grid_spec,
        out_shape=[jax.ShapeDtypeStruct((n_rows, D_MODEL), BF16), jax.ShapeDtypeStruct((n_rows, LANES), F32)],
        input_output_aliases={7: 0, 8: 1},
        compiler_params=pltpu.CompilerParams(dimension_semantics=("arbitrary",),
                                             vmem_limit_bytes=VMEM_LIMIT),
        name="dispatch",
    )(seg_len, seg_loc, seg_glob, tile_tot, x1b, pos_t, gate_t, rows0, gates0)


def _combine_kernel(len_ref, loc_ref, glob_ref, tot_ref,
                    pos_ref, x1_ref, lng_ref, lnb_ref, y_hbm, o_ref, ybuf_ref, acc_ref, sem):
    i = pl.program_id(0)
    nt = pl.num_programs(0)
    slot = i % 2
    ts = x1_ref.shape[0]

    def copies(s):
        def make(lo, go, n):
            return (pltpu.make_async_copy(y_hbm.at[pl.ds(go, n)], ybuf_ref.at[s, pl.ds(lo, n)], sem.at[s]),)
        return make

    tables = (len_ref, loc_ref, glob_ref)

    @pl.when(i == 0)
    def _():
        ybuf_ref[...] = jnp.zeros(ybuf_ref.shape, F32)
        _segment_copies(0, *tables, copies(0), lambda cp: cp.start())

    @pl.when(i + 1 < nt)
    def _():
        _segment_copies(i + 1, *tables, copies(1 - slot), lambda cp: cp.start())

    _segment_copies(i, *tables, copies(slot), lambda cp: cp.wait())

    acc_ref[...] = jnp.zeros((ts, D_MODEL), F32)

    def gather_rows(c, carry):
        r0 = pl.multiple_of(c * SORT_ROWS, SORT_ROWS)
        pcol = lax.broadcasted_iota(jnp.int32, (ts, SORT_ROWS), 1) + r0
        hit = None
        for k in range(TOP_K):
            hk = pcol == pos_ref[:, k:k + 1]
            hit = hk if hit is None else (hit | hk)
        onehot = jnp.where(hit, 1.0, 0.0).astype(BF16)
        y = ybuf_ref[slot, pl.ds(r0, SORT_ROWS), :]
        hi = y.astype(BF16)
        lo = (y - hi.astype(F32)).astype(BF16)
        acc_ref[...] += (jnp.dot(onehot, hi, preferred_element_type=F32)
                         + jnp.dot(onehot, lo, preferred_element_type=F32))
        return carry

    lax.fori_loop(0, (tot_ref[i] + SORT_ROWS - 1) // SORT_ROWS, gather_rows, 0)
    o_ref[...] = _layer_norm(DEEPNORM_ALPHA * x1_ref[...] + acc_ref[...], lng_ref[...], lnb_ref[...])


def _combine(seg_len, seg_loc, seg_glob, tile_tot, pos, x1, lng, lnb, y):
    T = x1.shape[0]
    ts = TOKEN_TILE
    grid_spec = pltpu.PrefetchScalarGridSpec(
        num_scalar_prefetch=4,
        grid=(T // ts,),
        in_specs=[pl.BlockSpec((ts, TOP_K), lambda i, *_: (i, 0)),
                  pl.BlockSpec((ts, D_MODEL), lambda i, *_: (i, 0)),
                  pl.BlockSpec((1, D_MODEL), lambda i, *_: (0, 0)),
                  pl.BlockSpec((1, D_MODEL), lambda i, *_: (0, 0)),
                  pl.BlockSpec(memory_space=pl.ANY)],
        out_specs=pl.BlockSpec((ts, D_MODEL), lambda i, *_: (i, 0)),
        scratch_shapes=[pltpu.VMEM((2, SORTED_ROWS, D_MODEL), F32),
                        pltpu.VMEM((ts, D_MODEL), F32),
                        pltpu.SemaphoreType.DMA((2,))],
    )
    return pl.pallas_call(
        _combine_kernel,
        grid_spec=grid_spec,
        out_shape=jax.ShapeDtypeStruct((T, D_MODEL), F32),
        compiler_params=pltpu.CompilerParams(dimension_semantics=("arbitrary",),
                                             vmem_limit_bytes=VMEM_LIMIT),
        name="combine",
    )(seg_len, seg_loc, seg_glob, tile_tot, pos, x1, lng, lnb, y)


def _moe(x1, x1b, gate_t, pos_t, cnt, wup, bup, wdn, bdn, lng, lnb):
    T = x1.shape[0]
    ts = TOKEN_TILE
    nt = T // ts
    r = EXPERT_ROWS
    n_blocks = -(-(T * TOP_K + nt * N_EXPERTS * (SEG_ALIGN - 1) + N_EXPERTS * (r - 1)) // r)
    counts = cnt[:, 0].reshape(nt, N_EXPERTS).astype(jnp.int32)
    seg_len = (counts + SEG_ALIGN - 1) // SEG_ALIGN * SEG_ALIGN
    seg_loc = jnp.cumsum(seg_len, axis=1) - seg_len
    tile_tot = jnp.sum(seg_len, axis=1)
    region = jnp.sum(seg_len, axis=0)
    region = (region + r - 1) // r * r
    region_end = jnp.cumsum(region)
    seg_glob = (region_end - region)[None, :] + jnp.cumsum(seg_len, axis=0) - seg_len
    n_used = (region_end[-1:] // r).astype(jnp.int32)
    block_expert = jnp.minimum(
        jnp.searchsorted(region_end, jnp.arange(n_blocks, dtype=jnp.int32) * r, side="right"),
        N_EXPERTS - 1).astype(jnp.int32)
    tables = (seg_len.reshape(-1), seg_loc.reshape(-1), seg_glob.reshape(-1), tile_tot)
    rows, row_gate = _dispatch(*tables, x1b, pos_t, gate_t, n_blocks * r)
    y = _experts(block_expert, n_used, rows, row_gate, wup, bup, wdn, bdn)
    return _combine(*tables, pos_t.T, x1, lng, lnb, y)


def kernel(x, mem, positions, a_w_in, a_b_in, a_dw, a_dw_b, a_cn_g, a_cn_b, a_w_out, a_b_out, w_kv_shared, b_w_q, b_w_out, b_b_out, mem_w_kv, ln_g, ln_b, router_w, router_b, exp_w_up, exp_b_up, exp_w_down, exp_b_down):
    B, S, D = x.shape
    T = B * S
    assert D == D_MODEL and S % DIL_TILE == 0 and S % TOKEN_TILE == 0
    xt = x.reshape(T, D)
    kst, vst = _mem_kv(mem.astype(BF16), mem_w_kv.astype(BF16))
    pos = positions.reshape(T, 1).astype(jnp.int32)
    half = ROT_DIM // 2
    inv_freq = jnp.power(ROPE_THETA, -jnp.arange(half, dtype=F32) / half)
    e = jnp.arange(LANES) % HEAD_DIM
    freq = jnp.where(e < ROT_DIM, inv_freq[e % half], 0.0).reshape(1, LANES).astype(F32)
    dw_pad = jnp.pad(a_dw, ((0, 0), (0, 32 - CONV_WIDTH), (0, 0)))
    shared_kv = None
    for l in range(DEPTH):
        wrt = router_w[l].T
        br = router_b[l].reshape(N_EXPERTS, 1)
        lng0, lnb0 = _row(ln_g[l, 0]), _row(ln_b[l, 0])
        if l < N_A_LAYERS:
            outs = _layer_a(xt, S, a_w_in[l].astype(BF16), _row(a_b_in[l]), dw_pad[l], _row(a_dw_b[l]),
                            _row(a_cn_g[l]), _row(a_cn_b[l]), kst[l], vst[l],
                            a_w_out[l].astype(BF16), _row(a_b_out[l]), lng0, lnb0, wrt, br)
        else:
            j = l - N_A_LAYERS
            if j == 0:
                q, qm, k, v = _layer_b_proj(xt, B, S, pos, freq, b_w_q[j].astype(BF16), w_kv_shared.astype(BF16))
                shared_kv = (k, v)
            else:
                q, qm = _layer_b_proj(xt, B, S, pos, freq, b_w_q[j].astype(BF16), None)
            dil = []
            for g, (window, dilation) in enumerate(DIL_GROUPS):
                assert window // dilation == DIL_BLOCK and DIL_TILE % (DIL_BLOCK * dilation) == 0
                dil += _dilated_group(g, dilation, q, shared_kv[0], shared_kv[1])
            outs = _layer_b_out(xt, S, dil, qm, kst[l], vst[l], b_w_out[j].astype(BF16), _row(b_b_out[j]),
                                lng0, lnb0, wrt, br)
        x1, x1b, gate_t, pos_t, cnt = outs
        xt = _moe(x1, x1b, gate_t, pos_t, cnt,
                  exp_w_up[l].astype(BF16), exp_b_up[l].reshape(N_EXPERTS, 1, 2 * D_FF),
                  exp_w_down[l].astype(BF16), exp_b_down[l].reshape(N_EXPERTS, 1, D_MODEL),
                  _row(ln_g[l, 1]), _row(ln_b[l, 1]))
    return xt.reshape(B, S, D)
```

```python
import functools

import jax
import jax.numpy as jnp
from jax import lax
from jax.experimental import pallas as pl
from jax.experimental.pallas import tpu as pltpu

D_MODEL = 1024
DEPTH = 4
N_A_LAYERS = 2
HEAD_DIM = 64
ROT_DIM = 16
ROPE_THETA = 500000.0
CONV_CH = 768
CONV_WIDTH = 31
MEM_TOKENS = 256
MEM_WIDTH = 256
DIL_GROUPS = ((128, 1), (512, 4), (2048, 16))
DIL_WIDTH = 256
DIL_BLOCK = 128
N_EXPERTS = 32
TOP_K = 4
D_FF = 1024
SWIGLU_ALPHA = 1.702
SWIGLU_LIMIT = 7.0
DEEPNORM_ALPHA = (2 * DEPTH) ** 0.25
LN_EPS = 1e-5
ATTN_SCALE = HEAD_DIM ** -0.5

LANES = 128
TOKEN_TILE = 512
DIL_TILE = 2048
EXPERT_ROWS = 512
CONV_ROWS = 32
SEG_ALIGN = 16
SEG_BITS = tuple(SEG_ALIGN << b for b in reversed(range(6)))
SORT_ROWS = 256
SORT_UNROLL = 2
SORTED_ROWS = 2560
VMEM_LIMIT = 56 * 1024 * 1024

F32 = jnp.float32
BF16 = jnp.bfloat16
NEG = -1e30


def _layer_norm(v, g, b):
    mu = jnp.mean(v, axis=-1, keepdims=True)
    c = v - mu
    var = jnp.mean(c * c, axis=-1, keepdims=True)
    return c * lax.rsqrt(var + LN_EPS) * g + b


def _nt_dot(a, b, **kw):
    return lax.dot_general(a, b, (((1,), (1,)), ((), ())), preferred_element_type=F32, **kw)


def _mem_attention(qm_bf16, kst_ref, vst_ref):
    out = None
    for h in range(4):
        s = _nt_dot(qm_bf16, kst_ref[h])
        m = jnp.max(s, axis=-1, keepdims=True)
        p = jnp.exp(s - m)
        l = jnp.sum(p, axis=-1, keepdims=True)
        pn = (p * (1.0 / l)).astype(BF16)
        u = jnp.dot(pn, vst_ref[h], preferred_element_type=F32)
        out = u if out is None else out + u
    return out


def _residual_ln_router(x_res, y, lng_ref, lnb_ref, wrt_ref, br_ref,
                        x1_ref, x1b_ref, gate_ref, pos_ref, cnt_ref):
    ts = x_res.shape[0]
    x1 = _layer_norm(DEEPNORM_ALPHA * x_res + y, lng_ref[...], lnb_ref[...])
    x1_ref[...] = x1
    x1b_ref[...] = x1.astype(BF16)
    logit = _nt_dot(wrt_ref[...], x1, precision=lax.Precision.HIGHEST) + br_ref[...]
    row = lax.broadcasted_iota(jnp.int32, (N_EXPERTS, ts), 0)
    vals, onehots = [], []
    for k in range(TOP_K):
        m = jnp.max(logit, axis=0, keepdims=True)
        idx = jnp.min(jnp.where(logit == m, row, N_EXPERTS), axis=0, keepdims=True)
        oh = row == idx
        logit = jnp.where(oh, -jnp.inf, logit)
        vals.append(m)
        onehots.append(oh)
    exps = [jnp.exp(v - vals[0]) for v in vals]
    inv = 1.0 / (exps[0] + exps[1] + exps[2] + exps[3])
    for k in range(TOP_K):
        gate_ref[k:k + 1, :] = exps[k] * inv
    oh_all = (onehots[0] | onehots[1] | onehots[2] | onehots[3])
    oh_f = jnp.where(oh_all, 1.0, 0.0)
    ri = lax.broadcasted_iota(jnp.int32, (ts, ts), 0)
    ci = lax.broadcasted_iota(jnp.int32, (ts, ts), 1)
    tri = jnp.where(ri < ci, 1.0, 0.0).astype(BF16)
    cum = jnp.dot(oh_f.astype(BF16), tri, preferred_element_type=F32)
    cnt = jnp.sum(oh_f, axis=1, keepdims=True)
    cnt_b = jnp.broadcast_to(cnt, (N_EXPERTS, LANES))
    cnt_ref[...] = cnt_b
    units = jnp.floor((cnt_b + (SEG_ALIGN - 1)) * (1.0 / SEG_ALIGN))
    er = lax.broadcasted_iota(jnp.int32, (N_EXPERTS, N_EXPERTS), 0)
    ec = lax.broadcasted_iota(jnp.int32, (N_EXPERTS, N_EXPERTS), 1)
    low_tri = jnp.where(ec < er, 1.0, 0.0).astype(BF16)
    seg_start = jnp.dot(low_tri, units.astype(BF16), preferred_element_type=F32)[:, 0:1] * SEG_ALIGN
    for k in range(TOP_K):
        rk = jnp.sum(jnp.where(onehots[k], cum + seg_start, 0.0), axis=0, keepdims=True)
        pos_ref[k:k + 1, :] = rk.astype(jnp.int32)


def _mem_kv_kernel(mem_ref, w_ref, kst_ref, vst_ref):
    kv = jnp.dot(mem_ref[...], w_ref[...], preferred_element_type=F32)
    k = kv[:, :MEM_WIDTH] * ATTN_SCALE
    v = kv[:, MEM_WIDTH:]
    head = lax.broadcasted_iota(jnp.int32, (MEM_TOKENS, MEM_WIDTH), 1) // HEAD_DIM
    for h in range(4):
        kst_ref[h] = jnp.where(head == h, k, 0.0).astype(BF16)
        vst_ref[h] = jnp.where(head == h, v, 0.0).astype(BF16)


def _mem_kv(mem_b, w_b):
    B = mem_b.shape[0]
    out = jax.ShapeDtypeStruct((DEPTH, B, 4, MEM_TOKENS, MEM_WIDTH), BF16)
    return pl.pallas_call(
        _mem_kv_kernel,
        grid=(DEPTH, B),
        in_specs=[pl.BlockSpec((None, MEM_TOKENS, D_MODEL), lambda l, b: (b, 0, 0)),
                  pl.BlockSpec((None, D_MODEL, 2 * MEM_WIDTH), lambda l, b: (l, 0, 0))],
        out_specs=[pl.BlockSpec((None, None, 4, MEM_TOKENS, MEM_WIDTH), lambda l, b: (l, b, 0, 0, 0))] * 2,
        out_shape=[out, out],
        name="mem_kv",
    )(mem_b, w_b)


def _layer_a_kernel(tiles_per_batch,
                    x_ref, win_ref, bin_ref, dw_ref, dwb_ref, cng_ref, cnb_ref, kst_ref, vst_ref,
                    wout_ref, bout_ref, lng_ref, lnb_ref, wrt_ref, br_ref,
                    x1_ref, x1b_ref, gate_ref, pos_ref, cnt_ref,
                    hpad_ref, shift_ref, cat_ref):
    ts = x_ref.shape[0]
    i = pl.program_id(0)
    x = x_ref[...]
    h = jnp.dot(x.astype(BF16), win_ref[...], preferred_element_type=F32) + bin_ref[...]
    hg = h[:, :CONV_CH] * jax.nn.sigmoid(h[:, CONV_CH:2 * CONV_CH])
    qm = h[:, 2 * CONV_CH:].astype(BF16)

    @pl.when(i % tiles_per_batch == 0)
    def _():
        hpad_ref[0:32, :] = jnp.zeros((32, CONV_CH), F32)

    hpad_ref[32:32 + ts, :] = hg
    for b in range(8):
        n = ts + 8 * ((CONV_WIDTH - 1 - b) // 8)
        for r0 in range(0, n, 128):
            rows = min(128, n - r0)
            shift_ref[b, r0:r0 + rows, :] = hpad_ref[pl.ds(2 + b + r0, rows), :]
    hpad_ref[0:32, :] = hpad_ref[ts:ts + 32, :]

    def conv_rows(c, carry):
        r0 = pl.multiple_of(c * CONV_ROWS, CONV_ROWS)
        acc = jnp.zeros((CONV_ROWS, CONV_CH), F32) + dwb_ref[...]
        for k in range(CONV_WIDTH):
            a, b = divmod(k, 8)
            acc = acc + shift_ref[b, pl.ds(r0 + 8 * a, CONV_ROWS), :] * dw_ref[k:k + 1, :]
        cn = _layer_norm(acc, cng_ref[...], cnb_ref[...])
        cat_ref[pl.ds(r0, CONV_ROWS), 0:CONV_CH] = (cn * jax.nn.sigmoid(cn)).astype(BF16)
        return carry

    lax.fori_loop(0, ts // CONV_ROWS, conv_rows, 0)

    cat_ref[:, CONV_CH:] = _mem_attention(qm, kst_ref, vst_ref).astype(BF16)
    y = jnp.dot(cat_ref[...], wout_ref[...], preferred_element_type=F32) + bout_ref[...]
    _residual_ln_router(x, y, lng_ref, lnb_ref, wrt_ref, br_ref,
                        x1_ref, x1b_ref, gate_ref, pos_ref, cnt_ref)


def _row(v):
    return v.reshape(1, -1).astype(F32)


def _const_spec(shape):
    nd = len(shape)
    return pl.BlockSpec(shape, lambda i: (0,) * nd)


def _router_out(T, ts):
    nt = T // ts
    shapes = [jax.ShapeDtypeStruct((T, D_MODEL), F32), jax.ShapeDtypeStruct((T, D_MODEL), BF16),
              jax.ShapeDtypeStruct((TOP_K, T), F32), jax.ShapeDtypeStruct((TOP_K, T), jnp.int32),
              jax.ShapeDtypeStruct((nt * N_EXPERTS, LANES), F32)]
    specs = [pl.BlockSpec((ts, D_MODEL), lambda i: (i, 0)), pl.BlockSpec((ts, D_MODEL), lambda i: (i, 0)),
             pl.BlockSpec((TOP_K, ts), lambda i: (0, i)), pl.BlockSpec((TOP_K, ts), lambda i: (0, i)),
             pl.BlockSpec((N_EXPERTS, LANES), lambda i: (i, 0))]
    return shapes, specs


def _layer_a(x, S, win, bin_, dw, dwb, cng, cnb, kst, vst, wout, bout, lng, lnb, wrt, br):
    T = x.shape[0]
    ts = TOKEN_TILE
    tpb = S // ts
    out_shapes, out_specs = _router_out(T, ts)
    kv_spec = pl.BlockSpec((None, 4, MEM_TOKENS, MEM_WIDTH), lambda i: (i // tpb, 0, 0, 0))
    a_in = 2 * CONV_CH + MEM_WIDTH
    return pl.pallas_call(
        functools.partial(_layer_a_kernel, tpb),
        grid=(T // ts,),
        in_specs=[pl.BlockSpec((ts, D_MODEL), lambda i: (i, 0)),
                  _const_spec((D_MODEL, a_in)), _const_spec((1, a_in)),
                  _const_spec((32, CONV_CH)), _const_spec((1, CONV_CH)),
                  _const_spec((1, CONV_CH)), _const_spec((1, CONV_CH)),
                  kv_spec, kv_spec,
                  _const_spec((D_MODEL, D_MODEL)), _const_spec((1, D_MODEL)),
                  _const_spec((1, D_MODEL)), _const_spec((1, D_MODEL)),
                  _const_spec((N_EXPERTS, D_MODEL)), _const_spec((N_EXPERTS, 1))],
        out_specs=out_specs,
        out_shape=out_shapes,
        scratch_shapes=[pltpu.VMEM((ts + 32, CONV_CH), F32),
                        pltpu.VMEM((8, ts + 24, CONV_CH), F32),
                        pltpu.VMEM((ts, D_MODEL), BF16)],
        compiler_params=pltpu.CompilerParams(dimension_semantics=("arbitrary",),
                                             vmem_limit_bytes=VMEM_LIMIT),
        name="layer_a",
    )(x, win, bin_, dw, dwb, cng, cnb, kst, vst, wout, bout, lng, lnb, wrt, br)


def _rotary(v, cos_t, sin_t, low):
    partner = jnp.where(low, pltpu.roll(v, LANES - 8, axis=1), pltpu.roll(v, 8, axis=1))
    return v * cos_t + partner * sin_t


def _rope_tables(pos_ref, freq_ref):
    e = lax.broadcasted_iota(jnp.int32, (1, LANES), 1) % HEAD_DIM
    ang = pos_ref[...].astype(F32) * freq_ref[...]
    low = e < ROT_DIM // 2
    rot = e < ROT_DIM
    cos_t = jnp.where(rot, jnp.cos(ang), 1.0)
    sn = jnp.sin(ang)
    sin_t = jnp.where(low, -sn, jnp.where(rot, sn, 0.0))
    return cos_t, sin_t, low


def _layer_b_proj_kernel(with_kv, x_ref, pos_ref, freq_ref, wq_ref, *rest):
    if with_kv:
        wkv_ref, q_ref, qm_ref, k_ref, v_ref = rest
    else:
        q_ref, qm_ref = rest
    xb = x_ref[...].astype(BF16)
    cos_t, sin_t, low = _rope_tables(pos_ref, freq_ref)
    q = jnp.dot(xb, wq_ref[...], preferred_element_type=F32)
    for g in range(3):
        for s in range(2):
            c0 = g * DIL_WIDTH + s * LANES
            q_ref[g, s] = _rotary(q[:, c0:c0 + LANES], cos_t, sin_t, low) * ATTN_SCALE
    qm_ref[...] = q[:, 3 * DIL_WIDTH:].astype(BF16)
    if with_kv:
        kv = jnp.dot(xb, wkv_ref[...], preferred_element_type=F32)
        for g in range(3):
            for s in range(2):
                c0 = g * 2 * DIL_WIDTH + s * LANES
                k_ref[g, s] = _rotary(kv[:, c0:c0 + LANES], cos_t, sin_t, low)
                v_ref[g, s] = kv[:, c0 + DIL_WIDTH:c0 + DIL_WIDTH + LANES]


def _layer_b_proj(x, B, S, pos, freq, wq, wkv):
    T = x.shape[0]
    ts = TOKEN_TILE
    tpb = S // ts
    with_kv = wkv is not None
    slab = jax.ShapeDtypeStruct((3, B, 2, S, LANES), F32)
    slab_spec = pl.BlockSpec((3, None, 2, ts, LANES), lambda i: (0, i // tpb, 0, i % tpb, 0))
    in_specs = [pl.BlockSpec((ts, D_MODEL), lambda i: (i, 0)),
                pl.BlockSpec((ts, 1), lambda i: (i, 0)),
                _const_spec((1, LANES)),
                _const_spec((D_MODEL, D_MODEL))]
    args = [x, pos, freq, wq]
    out_shape = [slab, jax.ShapeDtypeStruct((T, MEM_WIDTH), BF16)]
    out_specs = [slab_spec, pl.BlockSpec((ts, MEM_WIDTH), lambda i: (i, 0))]
    if with_kv:
        in_specs.append(_const_spec((D_MODEL, 6 * DIL_WIDTH)))
        args.append(wkv)
        out_shape += [slab, slab]
        out_specs += [slab_spec, slab_spec]
    return pl.pallas_call(
        functools.partial(_layer_b_proj_kernel, with_kv),
        grid=(T // ts,),
        in_specs=in_specs, out_specs=out_specs, out_shape=out_shape,
        compiler_params=pltpu.CompilerParams(dimension_semantics=("arbitrary",),
                                             vmem_limit_bytes=VMEM_LIMIT),
        name="layer_b_proj_kv" if with_kv else "layer_b_proj",
    )(*args)


def _dilated_kernel(d, q_ref, k_ref, v_ref, o_ref, lse_ref, kbuf_ref, vbuf_ref):
    i = pl.program_id(1)
    tile = DIL_TILE
    blk = DIL_BLOCK

    @pl.when(i == 0)
    def _():
        kbuf_ref[:, 0:tile, :] = jnp.zeros((2, tile, LANES), F32)
        vbuf_ref[:, 0:tile, :] = jnp.zeros((2, tile, LANES), F32)

    kbuf_ref[:, tile:, :] = k_ref[...]
    vbuf_ref[:, tile:, :] = v_ref[...]

    def rows(start, n):
        return pl.ds(start, n) if d == 1 else pl.ds(start, n, stride=d)

    qi = lax.broadcasted_iota(jnp.int32, (blk, 2 * blk), 0)
    ki = lax.broadcasted_iota(jnp.int32, (blk, 2 * blk), 1)
    dist = qi + blk - ki
    band = (dist >= 0) & (dist <= blk)
    head = lax.broadcasted_iota(jnp.int32, (1, LANES), 1) // HEAD_DIM

    def block(j, carry):
        c = j // d
        r = j % d
        qs = c * (blk * d) + r
        ks = tile + qs - blk * d
        first = jnp.logical_and(i == 0, c == 0)
        valid = band & jnp.logical_or(jnp.logical_not(first), ki >= blk)
        for s in range(2):
            q = q_ref[s, rows(qs, blk), :].astype(BF16)
            kk = kbuf_ref[s, rows(ks, 2 * blk), :]
            vv = vbuf_ref[s, rows(ks, 2 * blk), :]
            o_acc = jnp.zeros((blk, LANES), F32)
            lse_acc = jnp.zeros((blk, LANES), F32)
            for h in range(2):
                hm = head == h
                sc = _nt_dot(q, jnp.where(hm, kk, 0.0).astype(BF16))
                sc = jnp.where(valid, sc, NEG)
                m = jnp.max(sc, axis=-1, keepdims=True)
                p = jnp.exp(sc - m)
                l = jnp.sum(p, axis=-1, keepdims=True)
                u = jnp.dot(p.astype(BF16), jnp.where(hm, vv, 0.0).astype(BF16),
                            preferred_element_type=F32)
                o_acc = o_acc + u * (1.0 / l)
                lse_acc = jnp.where(hm, m + jnp.log(l), lse_acc)
            o_ref[s, rows(qs, blk), :] = o_acc
            lse_ref[s, rows(qs, blk), :] = lse_acc
        return carry

    lax.fori_loop(0, tile // blk, block, 0)
    kbuf_ref[:, 0:tile, :] = kbuf_ref[:, tile:, :]
    vbuf_ref[:, 0:tile, :] = vbuf_ref[:, tile:, :]


def _dilated_group(g, d, q, k, v):
    _, B, _, S, _ = q.shape
    tile = DIL_TILE
    in_spec = pl.BlockSpec((None, None, 2, tile, LANES), lambda b, i: (g, b, 0, i, 0))
    out_spec = pl.BlockSpec((None, 2, tile, LANES), lambda b, i: (b, 0, i, 0))
    out = jax.ShapeDtypeStruct((B, 2, S, LANES), F32)
    return pl.pallas_call(
        functools.partial(_dilated_kernel, d),
        grid=(B, S // tile),
        in_specs=[in_spec, in_spec, in_spec],
        out_specs=[out_spec, out_spec],
        out_shape=[out, out],
        scratch_shapes=[pltpu.VMEM((2, 2 * tile, LANES), F32), pltpu.VMEM((2, 2 * tile, LANES), F32)],
        compiler_params=pltpu.CompilerParams(dimension_semantics=("arbitrary", "arbitrary"),
                                             vmem_limit_bytes=VMEM_LIMIT),
        name=f"dilated_d{d}",
    )(q, k, v)


def _layer_b_out_kernel(x_ref, o0_ref, l0_ref, o1_ref, l1_ref, o2_ref, l2_ref, qm_ref, kst_ref, vst_ref,
                        wout_ref, bout_ref, lng_ref, lnb_ref, wrt_ref, br_ref,
                        x1_ref, x1b_ref, gate_ref, pos_ref, cnt_ref, cat_ref):
    for s in range(2):
        lses = [l0_ref[s], l1_ref[s], l2_ref[s]]
        outs = [o0_ref[s], o1_ref[s], o2_ref[s]]
        m = jnp.maximum(jnp.maximum(lses[0], lses[1]), lses[2])
        es = [jnp.exp(l - m) for l in lses]
        inv = 1.0 / (es[0] + es[1] + es[2])
        dil = (es[0] * inv) * outs[0] + (es[1] * inv) * outs[1] + (es[2] * inv) * outs[2]
        cat_ref[:, s * LANES:(s + 1) * LANES] = dil.astype(BF16)
    cat_ref[:, DIL_WIDTH:] = _mem_attention(qm_ref[...], kst_ref, vst_ref).astype(BF16)
    y = jnp.dot(cat_ref[...], wout_ref[...], preferred_element_type=F32) + bout_ref[...]
    _residual_ln_router(x_ref[...], y, lng_ref, lnb_ref, wrt_ref, br_ref,
                        x1_ref, x1b_ref, gate_ref, pos_ref, cnt_ref)


def _layer_b_out(x, S, dil, qm, kst, vst, wout, bout, lng, lnb, wrt, br):
    T = x.shape[0]
    ts = TOKEN_TILE
    tpb = S // ts
    out_shapes, out_specs = _router_out(T, ts)
    kv_spec = pl.BlockSpec((None, 4, MEM_TOKENS, MEM_WIDTH), lambda i: (i // tpb, 0, 0, 0))
    slab_spec = pl.BlockSpec((None, 2, ts, LANES), lambda i: (i // tpb, 0, i % tpb, 0))
    b_out = DIL_WIDTH + MEM_WIDTH
    return pl.pallas_call(
        _layer_b_out_kernel,
        grid=(T // ts,),
        in_specs=[pl.BlockSpec((ts, D_MODEL), lambda i: (i, 0))] + [slab_spec] * 6 + [
            pl.BlockSpec((ts, MEM_WIDTH), lambda i: (i, 0)), kv_spec, kv_spec,
            _const_spec((b_out, D_MODEL)), _const_spec((1, D_MODEL)),
            _const_spec((1, D_MODEL)), _const_spec((1, D_MODEL)),
            _const_spec((N_EXPERTS, D_MODEL)), _const_spec((N_EXPERTS, 1))],
        out_specs=out_specs,
        out_shape=out_shapes,
        scratch_shapes=[pltpu.VMEM((ts, b_out), BF16)],
        compiler_params=pltpu.CompilerParams(dimension_semantics=("arbitrary",),
                                             vmem_limit_bytes=VMEM_LIMIT),
        name="layer_b_out",
    )(x, *dil, qm, kst, vst, wout, bout, lng, lnb, wrt, br)


def _used_block(i, nu):
    return jnp.maximum(jnp.minimum(i, nu[0] - 1), 0)


def _expert_kernel(be_ref, nused_ref, rows_ref, gate_ref, wup_ref, bup_ref, wdn_ref, bdn_ref, y_ref,
                   wup_bf_ref, wdn_bf_ref):
    i = pl.program_id(0)
    used = i < nused_ref[0]

    @pl.when(jnp.logical_not(used))
    def _():
        y_ref[...] = jnp.zeros(y_ref.shape, y_ref.dtype)

    b = _used_block(i, nused_ref)
    new_expert = jnp.logical_or(i == 0, be_ref[b] != be_ref[jnp.maximum(b - 1, 0)])

    @pl.when(jnp.logical_and(used, new_expert))
    def _():
        wup_bf_ref[...] = wup_ref[...].astype(BF16)
        wdn_bf_ref[...] = wdn_ref[...].astype(BF16)

    @pl.when(used)
    def _():
        h = jnp.dot(rows_ref[...], wup_bf_ref[...], preferred_element_type=F32) + bup_ref[...]
        g = jnp.minimum(h[:, :D_FF], SWIGLU_LIMIT)
        lin = jnp.clip(h[:, D_FF:], -SWIGLU_LIMIT, SWIGLU_LIMIT)
        act = g * jax.nn.sigmoid(SWIGLU_ALPHA * g) * (lin + 1.0)
        y = jnp.dot(act.astype(BF16), wdn_bf_ref[...], preferred_element_type=F32) + bdn_ref[...]
        gate = gate_ref[...]
        for c in range(D_MODEL // LANES):
            y_ref[:, c * LANES:(c + 1) * LANES] = (y[:, c * LANES:(c + 1) * LANES] * gate).astype(y_ref.dtype)


def _experts(block_expert, n_used, rows, row_gate, wup, bup, wdn, bdn):
    nr = rows.shape[0]
    r = EXPERT_ROWS

    def blk(i, be, nu):
        return _used_block(i, nu)

    grid_spec = pltpu.PrefetchScalarGridSpec(
        num_scalar_prefetch=2,
        grid=(nr // r,),
        in_specs=[pl.BlockSpec((r, D_MODEL), lambda i, be, nu: (blk(i, be, nu), 0)),
                  pl.BlockSpec((r, LANES), lambda i, be, nu: (blk(i, be, nu), 0)),
                  pl.BlockSpec((None, D_MODEL, 2 * D_FF), lambda i, be, nu: (be[blk(i, be, nu)], 0, 0)),
                  pl.BlockSpec((None, 1, 2 * D_FF), lambda i, be, nu: (be[blk(i, be, nu)], 0, 0)),
                  pl.BlockSpec((None, D_FF, D_MODEL), lambda i, be, nu: (be[blk(i, be, nu)], 0, 0)),
                  pl.BlockSpec((None, 1, D_MODEL), lambda i, be, nu: (be[blk(i, be, nu)], 0, 0))],
        out_specs=pl.BlockSpec((r, D_MODEL), lambda i, be, nu: (i, 0)),
        scratch_shapes=[pltpu.VMEM((D_MODEL, 2 * D_FF), BF16), pltpu.VMEM((D_FF, D_MODEL), BF16)],
    )
    return pl.pallas_call(
        _expert_kernel,
        grid_spec=grid_spec,
        out_shape=jax.ShapeDtypeStruct((nr, D_MODEL), BF16),
        compiler_params=pltpu.CompilerParams(dimension_semantics=("arbitrary",),
                                             vmem_limit_bytes=VMEM_LIMIT),
        name="experts",
    )(block_expert, n_used, rows, row_gate, wup, bup, wdn, bdn)


def _sort_trips(n_rows):
    step = SORT_ROWS * SORT_UNROLL
    return (n_rows + step - 1) // step


def _segment_copies(tile, len_ref, loc_ref, glob_ref, make_copies, act):
    def per_expert(e, carry):
        n = len_ref[tile * N_EXPERTS + e]
        loc = loc_ref[tile * N_EXPERTS + e]
        glob = glob_ref[tile * N_EXPERTS + e]
        off = jnp.int32(0)
        for bit in SEG_BITS:
            part = n & bit

            @pl.when(part != 0)
            def _():
                lo = pl.multiple_of(loc + off, SEG_ALIGN)
                go = pl.multiple_of(glob + off, SEG_ALIGN)
                for cp in make_copies(lo, go, bit):
                    act(cp)

            off = off + part
        return carry

    lax.fori_loop(0, N_EXPERTS, per_expert, 0)


def _dispatch_kernel(len_ref, loc_ref, glob_ref, tot_ref,
                     xb_ref, pos_ref, gate_ref, rows_init, gates_init,
                     rows_hbm, gates_hbm, xs_ref, gs_ref, sem):
    del rows_init, gates_init
    i = pl.program_id(0)
    nt = pl.num_programs(0)
    slot = i % 2
    ts = xb_ref.shape[0]

    def copies(s):
        def make(lo, go, n):
            return (pltpu.make_async_copy(xs_ref.at[s, pl.ds(lo, n)], rows_hbm.at[pl.ds(go, n)], sem.at[0, s]),
                    pltpu.make_async_copy(gs_ref.at[s, pl.ds(lo, n)], gates_hbm.at[pl.ds(go, n)], sem.at[1, s]))
        return make

    tables = (len_ref, loc_ref, glob_ref)

    @pl.when(i >= 2)
    def _():
        _segment_copies(i - 2, *tables, copies(slot), lambda cp: cp.wait())

    xb = xb_ref[...]

    def sort_rows(c, carry):
        for u in range(SORT_UNROLL):
            r0 = pl.multiple_of((c * SORT_UNROLL + u) * SORT_ROWS, SORT_ROWS)
            prow = lax.broadcasted_iota(jnp.int32, (SORT_ROWS, ts), 0) + r0
            hit = None
            wg = jnp.zeros((SORT_ROWS, ts), F32)
            for k in range(TOP_K):
                hk = prow == pos_ref[k:k + 1, :]
                hit = hk if hit is None else (hit | hk)
                wg = jnp.where(hk, gate_ref[k:k + 1, :], wg)
            onehot = jnp.where(hit, 1.0, 0.0).astype(BF16)
            xs_ref[slot, pl.ds(r0, SORT_ROWS), :] = jnp.dot(onehot, xb, preferred_element_type=F32).astype(BF16)
            g = jnp.sum(wg, axis=1, keepdims=True)
            gs_ref[slot, pl.ds(r0, SORT_ROWS), :] = jnp.broadcast_to(g, (SORT_ROWS, LANES))
        return carry

    lax.fori_loop(0, _sort_trips(tot_ref[i]), sort_rows, 0)
    _segment_copies(i, *tables, copies(slot), lambda cp: cp.start())

    @pl.when(i == nt - 1)
    def _():
        @pl.when(i >= 1)
        def _():
            _segment_copies(i - 1, *tables, copies(1 - slot), lambda cp: cp.wait())
        _segment_copies(i, *tables, copies(slot), lambda cp: cp.wait())


def _dispatch(seg_len, seg_loc, seg_glob, tile_tot, x1b, pos_t, gate_t, n_rows):
    T = x1b.shape[0]
    ts = TOKEN_TILE
    any_spec = pl.BlockSpec(memory_space=pl.ANY)
    grid_spec = pltpu.PrefetchScalarGridSpec(
        num_scalar_prefetch=4,
        grid=(T // ts,),
        in_specs=[pl.BlockSpec((ts, D_MODEL), lambda i, *_: (i, 0)),
                  pl.BlockSpec((TOP_K, ts), lambda i, *_: (0, i)),
                  pl.BlockSpec((TOP_K, ts), lambda i, *_: (0, i)),
                  any_spec, any_spec],
        out_specs=[any_spec, any_spec],
        scratch_shapes=[pltpu.VMEM((2, SORTED_ROWS, D_MODEL), BF16),
                        pltpu.VMEM((2, SORTED_ROWS, LANES), F32),
                        pltpu.SemaphoreType.DMA((2, 2))],
    )
    rows0 = jnp.zeros((n_rows, D_MODEL), BF16)
    gates0 = jnp.zeros((n_rows, LANES), F32)
    return pl.pallas_call(
        _dispatch_kernel,
        grid_spec=grid_spec,
        out_shape=[jax.ShapeDtypeStruct((n_rows, D_MODEL), BF16), jax.ShapeDtypeStruct((n_rows, LANES), F32)],
        input_output_aliases={7: 0, 8: 1},
        compiler_params=pltpu.CompilerParams(dimension_semantics=("arbitrary",),
                                             vmem_limit_bytes=VMEM_LIMIT),
        name="dispatch",
    )(seg_len, seg_loc, seg_glob, tile_tot, x1b, pos_t, gate_t, rows0, gates0)


def _combine_kernel(len_ref, loc_ref, glob_ref, tot_ref,
                    pos_ref, x1_ref, lng_ref, lnb_ref, y_hbm, o_ref, ybuf_ref, posb_ref, acc_ref, sem):
    i = pl.program_id(0)
    nt = pl.num_programs(0)
    slot = i % 2
    ts = x1_ref.shape[0]

    def copies(s):
        def make(lo, go, n):
            return (pltpu.make_async_copy(y_hbm.at[pl.ds(go, n)], ybuf_ref.at[s, pl.ds(lo, n)], sem.at[s]),)
        return make

    tables = (len_ref, loc_ref, glob_ref)

    @pl.when(i == 0)
    def _():
        ybuf_ref[...] = jnp.zeros(ybuf_ref.shape, ybuf_ref.dtype)
        _segment_copies(0, *tables, copies(0), lambda cp: cp.start())

    @pl.when(i + 1 < nt)
    def _():
        _segment_copies(i + 1, *tables, copies(1 - slot), lambda cp: cp.start())

    for k in range(TOP_K):
        posb_ref[k] = jnp.broadcast_to(pos_ref[:, k:k + 1], (ts, LANES))
    acc_ref[...] = jnp.zeros((ts, D_MODEL), F32)
    _segment_copies(i, *tables, copies(slot), lambda cp: cp.wait())
    lane = lax.broadcasted_iota(jnp.int32, (ts, LANES), 1)

    def gather_rows(c, carry):
        for u in range(SORT_UNROLL):
            r0 = pl.multiple_of((c * SORT_UNROLL + u) * SORT_ROWS, SORT_ROWS)
            halves = []
            for half in range(SORT_ROWS // LANES):
                col = lane + (r0 + half * LANES)
                hit = None
                for k in range(TOP_K):
                    hk = posb_ref[k] == col
                    hit = hk if hit is None else (hit | hk)
                halves.append(jnp.where(hit, 1.0, 0.0).astype(BF16))
            onehot = jnp.concatenate(halves, axis=1)
            acc_ref[...] += jnp.dot(onehot, ybuf_ref[slot, pl.ds(r0, SORT_ROWS), :],
                                    preferred_element_type=F32)
        return carry

    lax.fori_loop(0, _sort_trips(tot_ref[i]), gather_rows, 0)
    o_ref[...] = _layer_norm(DEEPNORM_ALPHA * x1_ref[...] + acc_ref[...], lng_ref[...], lnb_ref[...])


def _combine(seg_len, seg_loc, seg_glob, tile_tot, pos, x1, lng, lnb, y):
    T = x1.shape[0]
    ts = TOKEN_TILE
    grid_spec = pltpu.PrefetchScalarGridSpec(
        num_scalar_prefetch=4,
        grid=(T // ts,),
        in_specs=[pl.BlockSpec((ts, TOP_K), lambda i, *_: (i, 0)),
                  pl.BlockSpec((ts, D_MODEL), lambda i, *_: (i, 0)),
                  pl.BlockSpec((1, D_MODEL), lambda i, *_: (0, 0)),
                  pl.BlockSpec((1, D_MODEL), lambda i, *_: (0, 0)),
                  pl.BlockSpec(memory_space=pl.ANY)],
        out_specs=pl.BlockSpec((ts, D_MODEL), lambda i, *_: (i, 0)),
        scratch_shapes=[pltpu.VMEM((2, SORTED_ROWS, D_MODEL), BF16),
                        pltpu.VMEM((TOP_K, ts, LANES), jnp.int32),
                        pltpu.VMEM((ts, D_MODEL), F32),
                        pltpu.SemaphoreType.DMA((2,))],
    )
    return pl.pallas_call(
        _combine_kernel,
        grid_spec=grid_spec,
        out_shape=jax.ShapeDtypeStruct((T, D_MODEL), F32),
        compiler_params=pltpu.CompilerParams(dimension_semantics=("arbitrary",),
                                             vmem_limit_bytes=VMEM_LIMIT),
        name="combine",
    )(seg_len, seg_loc, seg_glob, tile_tot, pos, x1, lng, lnb, y)


def _moe(x1, x1b, gate_t, pos_t, cnt, wup, bup, wdn, bdn, lng, lnb):
    T = x1.shape[0]
    ts = TOKEN_TILE
    nt = T // ts
    r = EXPERT_ROWS
    n_blocks = -(-(T * TOP_K + nt * N_EXPERTS * (SEG_ALIGN - 1) + N_EXPERTS * (r - 1)) // r)
    counts = cnt[:, 0].reshape(nt, N_EXPERTS).astype(jnp.int32)
    seg_len = (counts + SEG_ALIGN - 1) // SEG_ALIGN * SEG_ALIGN
    seg_loc = jnp.cumsum(seg_len, axis=1) - seg_len
    tile_tot = jnp.sum(seg_len, axis=1)
    region = jnp.sum(seg_len, axis=0)
    region = (region + r - 1) // r * r
    region_end = jnp.cumsum(region)
    seg_glob = (region_end - region)[None, :] + jnp.cumsum(seg_len, axis=0) - seg_len
    n_used = (region_end[-1:] // r).astype(jnp.int32)
    block_start = jnp.arange(n_blocks, dtype=jnp.int32) * r
    block_expert = jnp.minimum(jnp.sum(region_end[None, :] <= block_start[:, None], axis=1),
                               N_EXPERTS - 1).astype(jnp.int32)
    tables = (seg_len.reshape(-1), seg_loc.reshape(-1), seg_glob.reshape(-1), tile_tot)
    rows, row_gate = _dispatch(*tables, x1b, pos_t, gate_t, n_blocks * r)
    y = _experts(block_expert, n_used, rows, row_gate, wup, bup, wdn, bdn)
    return _combine(*tables, pos_t.T, x1, lng, lnb, y)


def kernel(x, mem, positions, a_w_in, a_b_in, a_dw, a_dw_b, a_cn_g, a_cn_b, a_w_out, a_b_out, w_kv_shared, b_w_q, b_w_out, b_b_out, mem_w_kv, ln_g, ln_b, router_w, router_b, exp_w_up, exp_b_up, exp_w_down, exp_b_down):
    B, S, D = x.shape
    T = B * S
    assert D == D_MODEL and S % DIL_TILE == 0 and S % TOKEN_TILE == 0
    xt = x.reshape(T, D)
    kst, vst = _mem_kv(mem.astype(BF16), mem_w_kv.astype(BF16))
    pos = positions.reshape(T, 1).astype(jnp.int32)
    half = ROT_DIM // 2
    inv_freq = jnp.power(ROPE_THETA, -jnp.arange(half, dtype=F32) / half)
    e = jnp.arange(LANES) % HEAD_DIM
    freq = jnp.where(e < ROT_DIM, inv_freq[e % half], 0.0).reshape(1, LANES).astype(F32)
    dw_pad = jnp.pad(a_dw, ((0, 0), (0, 32 - CONV_WIDTH), (0, 0)))
    shared_kv = None
    for l in range(DEPTH):
        wrt = router_w[l].T
        br = router_b[l].reshape(N_EXPERTS, 1)
        lng0, lnb0 = _row(ln_g[l, 0]), _row(ln_b[l, 0])
        if l < N_A_LAYERS:
            outs = _layer_a(xt, S, a_w_in[l].astype(BF16), _row(a_b_in[l]), dw_pad[l], _row(a_dw_b[l]),
                            _row(a_cn_g[l]), _row(a_cn_b[l]), kst[l], vst[l],
                            a_w_out[l].astype(BF16), _row(a_b_out[l]), lng0, lnb0, wrt, br)
        else:
            j = l - N_A_LAYERS
            if j == 0:
                q, qm, k, v = _layer_b_proj(xt, B, S, pos, freq, b_w_q[j].astype(BF16), w_kv_shared.astype(BF16))
                shared_kv = (k, v)
            else:
                q, qm = _layer_b_proj(xt, B, S, pos, freq, b_w_q[j].astype(BF16), None)
            dil = []
            for g, (window, dilation) in enumerate(DIL_GROUPS):
                assert window // dilation == DIL_BLOCK and DIL_TILE % (DIL_BLOCK * dilation) == 0
                dil += _dilated_group(g, dilation, q, shared_kv[0], shared_kv[1])
            outs = _layer_b_out(xt, S, dil, qm, kst[l], vst[l], b_w_out[j].astype(BF16), _row(b_b_out[j]),
                                lng0, lnb0, wrt, br)
        x1, x1b, gate_t, pos_t, cnt = outs
        xt = _moe(x1, x1b, gate_t, pos_t, cnt,
                  exp_w_up[l], exp_b_up[l].reshape(N_EXPERTS, 1, 2 * D_FF),
                  exp_w_down[l], exp_b_down[l].reshape(N_EXPERTS, 1, D_MODEL),
                  _row(ln_g[l, 1]), _row(ln_b[l, 1]))
    return xt.reshape(B, S, D)
```

```python
import functools

import jax
import jax.numpy as jnp
from jax import lax
from jax.experimental import pallas as pl
from jax.experimental.pallas import tpu as pltpu

D_MODEL = 1024
DEPTH = 4
N_A_LAYERS = 2
HEAD_DIM = 64
ROT_DIM = 16
ROPE_THETA = 500000.0
CONV_CH = 768
CONV_WIDTH = 31
MEM_TOKENS = 256
MEM_WIDTH = 256
DIL_GROUPS = ((128, 1), (512, 4), (2048, 16))
DIL_WIDTH = 256
DIL_BLOCK = 128
N_EXPERTS = 32
TOP_K = 4
D_FF = 1024
SWIGLU_ALPHA = 1.702
SWIGLU_LIMIT = 7.0
DEEPNORM_ALPHA = (2 * DEPTH) ** 0.25
LN_EPS = 1e-5
ATTN_SCALE = HEAD_DIM ** -0.5

LANES = 128
TOKEN_TILE = 512
DIL_TILE = 2048
EXPERT_ROWS = 512
EXPERT_SPLIT = 2
CONV_ROWS = 64
CONV_LANES = 384
SEG_ALIGN = 16
SEG_BITS = tuple(SEG_ALIGN << b for b in reversed(range(6)))
SORT_ROWS = 256
SORT_UNROLL = 2
SORTED_ROWS = 2560
VMEM_LIMIT = 56 * 1024 * 1024

F32 = jnp.float32
BF16 = jnp.bfloat16
NEG = -1e30


def _layer_norm(v, g, b):
    mu = jnp.mean(v, axis=-1, keepdims=True)
    c = v - mu
    var = jnp.mean(c * c, axis=-1, keepdims=True)
    return c * lax.rsqrt(var + LN_EPS) * g + b


def _nt_dot(a, b, **kw):
    return lax.dot_general(a, b, (((1,), (1,)), ((), ())), preferred_element_type=F32, **kw)


def _mem_attention(qm_bf16, kst_ref, vst_ref):
    out = None
    for h in range(4):
        s = _nt_dot(qm_bf16, kst_ref[h])
        m = jnp.max(s, axis=-1, keepdims=True)
        p = jnp.exp(s - m)
        l = jnp.sum(p, axis=-1, keepdims=True)
        pn = (p * (1.0 / l)).astype(BF16)
        u = jnp.dot(pn, vst_ref[h], preferred_element_type=F32)
        out = u if out is None else out + u
    return out


def _residual_ln_router(x_res, y, lng_ref, lnb_ref, wrt_ref, br_ref,
                        x1_ref, x1b_ref, gate_ref, pos_ref, cnt_ref):
    ts = x_res.shape[0]
    x1 = _layer_norm(DEEPNORM_ALPHA * x_res + y, lng_ref[...], lnb_ref[...])
    x1_ref[...] = x1
    x1b_ref[...] = x1.astype(BF16)
    logit = _nt_dot(wrt_ref[...], x1, precision=lax.Precision.HIGHEST) + br_ref[...]
    row = lax.broadcasted_iota(jnp.int32, (N_EXPERTS, ts), 0)
    vals, onehots = [], []
    for k in range(TOP_K):
        m = jnp.max(logit, axis=0, keepdims=True)
        idx = jnp.min(jnp.where(logit == m, row, N_EXPERTS), axis=0, keepdims=True)
        oh = row == idx
        logit = jnp.where(oh, -jnp.inf, logit)
        vals.append(m)
        onehots.append(oh)
    exps = [jnp.exp(v - vals[0]) for v in vals]
    inv = 1.0 / (exps[0] + exps[1] + exps[2] + exps[3])
    for k in range(TOP_K):
        gate_ref[k:k + 1, :] = exps[k] * inv
    oh_all = (onehots[0] | onehots[1] | onehots[2] | onehots[3])
    oh_f = jnp.where(oh_all, 1.0, 0.0)
    ri = lax.broadcasted_iota(jnp.int32, (ts, ts), 0)
    ci = lax.broadcasted_iota(jnp.int32, (ts, ts), 1)
    tri = jnp.where(ri < ci, 1.0, 0.0).astype(BF16)
    cum = jnp.dot(oh_f.astype(BF16), tri, preferred_element_type=F32)
    cnt = jnp.sum(oh_f, axis=1, keepdims=True)
    cnt_b = jnp.broadcast_to(cnt, (N_EXPERTS, LANES))
    cnt_ref[...] = cnt_b
    units = jnp.floor((cnt_b + (SEG_ALIGN - 1)) * (1.0 / SEG_ALIGN))
    er = lax.broadcasted_iota(jnp.int32, (N_EXPERTS, N_EXPERTS), 0)
    ec = lax.broadcasted_iota(jnp.int32, (N_EXPERTS, N_EXPERTS), 1)
    low_tri = jnp.where(ec < er, 1.0, 0.0).astype(BF16)
    seg_start = jnp.dot(low_tri, units.astype(BF16), preferred_element_type=F32)[:, 0:1] * SEG_ALIGN
    for k in range(TOP_K):
        rk = jnp.sum(jnp.where(onehots[k], cum + seg_start, 0.0), axis=0, keepdims=True)
        pos_ref[k:k + 1, :] = rk.astype(jnp.int32)


def _mem_kv_kernel(mem_ref, w_ref, kst_ref, vst_ref):
    kv = jnp.dot(mem_ref[...], w_ref[...], preferred_element_type=F32)
    k = kv[:, :MEM_WIDTH] * ATTN_SCALE
    v = kv[:, MEM_WIDTH:]
    head = lax.broadcasted_iota(jnp.int32, (MEM_TOKENS, MEM_WIDTH), 1) // HEAD_DIM
    for h in range(4):
        kst_ref[h] = jnp.where(head == h, k, 0.0).astype(BF16)
        vst_ref[h] = jnp.where(head == h, v, 0.0).astype(BF16)


def _mem_kv(mem_b, w_b):
    B = mem_b.shape[0]
    out = jax.ShapeDtypeStruct((DEPTH, B, 4, MEM_TOKENS, MEM_WIDTH), BF16)
    return pl.pallas_call(
        _mem_kv_kernel,
        grid=(DEPTH, B),
        in_specs=[pl.BlockSpec((None, MEM_TOKENS, D_MODEL), lambda l, b: (b, 0, 0)),
                  pl.BlockSpec((None, D_MODEL, 2 * MEM_WIDTH), lambda l, b: (l, 0, 0))],
        out_specs=[pl.BlockSpec((None, None, 4, MEM_TOKENS, MEM_WIDTH), lambda l, b: (l, b, 0, 0, 0))] * 2,
        out_shape=[out, out],
        name="mem_kv",
    )(mem_b, w_b)


def _layer_a_kernel(tiles_per_batch,
                    x_ref, win_ref, bin_ref, dw_ref, dwb_ref, cng_ref, cnb_ref, kst_ref, vst_ref,
                    wout_ref, bout_ref, lng_ref, lnb_ref, wrt_ref, br_ref,
                    x1_ref, x1b_ref, gate_ref, pos_ref, cnt_ref,
                    hpad_ref, shift_ref, cat_ref):
    ts = x_ref.shape[0]
    i = pl.program_id(0)
    x = x_ref[...]
    h = jnp.dot(x.astype(BF16), win_ref[...], preferred_element_type=F32) + bin_ref[...]
    hg = h[:, :CONV_CH] * jax.nn.sigmoid(h[:, CONV_CH:2 * CONV_CH])
    qm = h[:, 2 * CONV_CH:].astype(BF16)

    @pl.when(i % tiles_per_batch == 0)
    def _():
        hpad_ref[0:32, :] = jnp.zeros((32, CONV_CH), F32)

    hpad_ref[32:32 + ts, :] = hg
    for b in range(8):
        n = ts + 8 * ((CONV_WIDTH - 1 - b) // 8)
        for r0 in range(0, n, 128):
            rows = min(128, n - r0)
            shift_ref[b, r0:r0 + rows, :] = hpad_ref[pl.ds(2 + b + r0, rows), :]
    hpad_ref[0:32, :] = hpad_ref[ts:ts + 32, :]

    for c0 in range(0, CONV_CH, CONV_LANES):
        lanes = slice(c0, c0 + CONV_LANES)

        def conv_rows(c, carry, lanes=lanes):
            r0 = pl.multiple_of(c * CONV_ROWS, CONV_ROWS)
            acc = jnp.zeros((CONV_ROWS, CONV_LANES), F32) + dwb_ref[:, lanes]
            for k in range(CONV_WIDTH):
                a, b = divmod(k, 8)
                acc = acc + shift_ref[b, pl.ds(r0 + 8 * a, CONV_ROWS), lanes] * dw_ref[k:k + 1, lanes]
            hpad_ref[pl.ds(32 + r0, CONV_ROWS), lanes] = acc
            return carry

        lax.fori_loop(0, ts // CONV_ROWS, conv_rows, 0)

    def norm_rows(c, carry):
        r0 = pl.multiple_of(c * CONV_ROWS, CONV_ROWS)
        cn = _layer_norm(hpad_ref[pl.ds(32 + r0, CONV_ROWS), :], cng_ref[...], cnb_ref[...])
        cat_ref[pl.ds(r0, CONV_ROWS), 0:CONV_CH] = (cn * jax.nn.sigmoid(cn)).astype(BF16)
        return carry

    lax.fori_loop(0, ts // CONV_ROWS, norm_rows, 0)

    cat_ref[:, CONV_CH:] = _mem_attention(qm, kst_ref, vst_ref).astype(BF16)
    y = jnp.dot(cat_ref[...], wout_ref[...], preferred_element_type=F32) + bout_ref[...]
    _residual_ln_router(x, y, lng_ref, lnb_ref, wrt_ref, br_ref,
                        x1_ref, x1b_ref, gate_ref, pos_ref, cnt_ref)


def _row(v):
    return v.reshape(1, -1).astype(F32)


def _const_spec(shape):
    nd = len(shape)
    return pl.BlockSpec(shape, lambda i: (0,) * nd)


def _router_out(T, ts):
    nt = T // ts
    shapes = [jax.ShapeDtypeStruct((T, D_MODEL), F32), jax.ShapeDtypeStruct((T, D_MODEL), BF16),
              jax.ShapeDtypeStruct((TOP_K, T), F32), jax.ShapeDtypeStruct((TOP_K, T), jnp.int32),
              jax.ShapeDtypeStruct((nt * N_EXPERTS, LANES), F32)]
    specs = [pl.BlockSpec((ts, D_MODEL), lambda i: (i, 0)), pl.BlockSpec((ts, D_MODEL), lambda i: (i, 0)),
             pl.BlockSpec((TOP_K, ts), lambda i: (0, i)), pl.BlockSpec((TOP_K, ts), lambda i: (0, i)),
             pl.BlockSpec((N_EXPERTS, LANES), lambda i: (i, 0))]
    return shapes, specs


def _layer_a(x, S, win, bin_, dw, dwb, cng, cnb, kst, vst, wout, bout, lng, lnb, wrt, br):
    T = x.shape[0]
    ts = TOKEN_TILE
    tpb = S // ts
    out_shapes, out_specs = _router_out(T, ts)
    kv_spec = pl.BlockSpec((None, 4, MEM_TOKENS, MEM_WIDTH), lambda i: (i // tpb, 0, 0, 0))
    a_in = 2 * CONV_CH + MEM_WIDTH
    return pl.pallas_call(
        functools.partial(_layer_a_kernel, tpb),
        grid=(T // ts,),
        in_specs=[pl.BlockSpec((ts, D_MODEL), lambda i: (i, 0)),
                  _const_spec((D_MODEL, a_in)), _const_spec((1, a_in)),
                  _const_spec((32, CONV_CH)), _const_spec((1, CONV_CH)),
                  _const_spec((1, CONV_CH)), _const_spec((1, CONV_CH)),
                  kv_spec, kv_spec,
                  _const_spec((D_MODEL, D_MODEL)), _const_spec((1, D_MODEL)),
                  _const_spec((1, D_MODEL)), _const_spec((1, D_MODEL)),
                  _const_spec((N_EXPERTS, D_MODEL)), _const_spec((N_EXPERTS, 1))],
        out_specs=out_specs,
        out_shape=out_shapes,
        scratch_shapes=[pltpu.VMEM((ts + 32, CONV_CH), F32),
                        pltpu.VMEM((8, ts + 24, CONV_CH), F32),
                        pltpu.VMEM((ts, D_MODEL), BF16)],
        compiler_params=pltpu.CompilerParams(dimension_semantics=("arbitrary",),
                                             vmem_limit_bytes=VMEM_LIMIT),
        name="layer_a",
    )(x, win, bin_, dw, dwb, cng, cnb, kst, vst, wout, bout, lng, lnb, wrt, br)


def _rotary(v, cos_t, sin_t, low):
    partner = jnp.where(low, pltpu.roll(v, LANES - 8, axis=1), pltpu.roll(v, 8, axis=1))
    return v * cos_t + partner * sin_t


def _rope_tables(pos_ref, freq_ref):
    e = lax.broadcasted_iota(jnp.int32, (1, LANES), 1) % HEAD_DIM
    ang = pos_ref[...].astype(F32) * freq_ref[...]
    low = e < ROT_DIM // 2
    rot = e < ROT_DIM
    cos_t = jnp.where(rot, jnp.cos(ang), 1.0)
    sn = jnp.sin(ang)
    sin_t = jnp.where(low, -sn, jnp.where(rot, sn, 0.0))
    return cos_t, sin_t, low


def _layer_b_proj_kernel(with_kv, x_ref, pos_ref, freq_ref, wq_ref, *rest):
    if with_kv:
        wkv_ref, q_ref, qm_ref, k_ref, v_ref = rest
    else:
        q_ref, qm_ref = rest
    xb = x_ref[...].astype(BF16)
    cos_t, sin_t, low = _rope_tables(pos_ref, freq_ref)
    q = jnp.dot(xb, wq_ref[...], preferred_element_type=F32)
    for g in range(3):
        for s in range(2):
            c0 = g * DIL_WIDTH + s * LANES
            q_ref[g, s] = _rotary(q[:, c0:c0 + LANES], cos_t, sin_t, low) * ATTN_SCALE
    qm_ref[...] = q[:, 3 * DIL_WIDTH:].astype(BF16)
    if with_kv:
        kv = jnp.dot(xb, wkv_ref[...], preferred_element_type=F32)
        for g in range(3):
            for s in range(2):
                c0 = g * 2 * DIL_WIDTH + s * LANES
                k_ref[g, s] = _rotary(kv[:, c0:c0 + LANES], cos_t, sin_t, low)
                v_ref[g, s] = kv[:, c0 + DIL_WIDTH:c0 + DIL_WIDTH + LANES]


def _layer_b_proj(x, B, S, pos, freq, wq, wkv):
    T = x.shape[0]
    ts = TOKEN_TILE
    tpb = S // ts
    with_kv = wkv is not None
    slab = jax.ShapeDtypeStruct((3, B, 2, S, LANES), F32)
    slab_spec = pl.BlockSpec((3, None, 2, ts, LANES), lambda i: (0, i // tpb, 0, i % tpb, 0))
    in_specs = [pl.BlockSpec((ts, D_MODEL), lambda i: (i, 0)),
                pl.BlockSpec((ts, 1), lambda i: (i, 0)),
                _const_spec((1, LANES)),
                _const_spec((D_MODEL, D_MODEL))]
    args = [x, pos, freq, wq]
    out_shape = [slab, jax.ShapeDtypeStruct((T, MEM_WIDTH), BF16)]
    out_specs = [slab_spec, pl.BlockSpec((ts, MEM_WIDTH), lambda i: (i, 0))]
    if with_kv:
        in_specs.append(_const_spec((D_MODEL, 6 * DIL_WIDTH)))
        args.append(wkv)
        out_shape += [slab, slab]
        out_specs += [slab_spec, slab_spec]
    return pl.pallas_call(
        functools.partial(_layer_b_proj_kernel, with_kv),
        grid=(T // ts,),
        in_specs=in_specs, out_specs=out_specs, out_shape=out_shape,
        compiler_params=pltpu.CompilerParams(dimension_semantics=("arbitrary",),
                                             vmem_limit_bytes=VMEM_LIMIT),
        name="layer_b_proj_kv" if with_kv else "layer_b_proj",
    )(*args)


def _dilated_kernel(d, q_ref, k_ref, v_ref, o_ref, lse_ref, kbuf_ref, vbuf_ref):
    i = pl.program_id(1)
    tile = DIL_TILE
    blk = DIL_BLOCK

    @pl.when(i == 0)
    def _():
        kbuf_ref[:, 0:tile, :] = jnp.zeros((2, tile, LANES), F32)
        vbuf_ref[:, 0:tile, :] = jnp.zeros((2, tile, LANES), F32)

    kbuf_ref[:, tile:, :] = k_ref[...]
    vbuf_ref[:, tile:, :] = v_ref[...]

    def rows(start, n):
        return pl.ds(start, n) if d == 1 else pl.ds(start, n, stride=d)

    qi = lax.broadcasted_iota(jnp.int32, (blk, 2 * blk), 0)
    ki = lax.broadcasted_iota(jnp.int32, (blk, 2 * blk), 1)
    dist = qi + blk - ki
    band = (dist >= 0) & (dist <= blk)
    head = lax.broadcasted_iota(jnp.int32, (1, LANES), 1) // HEAD_DIM

    def block(j, carry):
        c = j // d
        r = j % d
        qs = c * (blk * d) + r
        ks = tile + qs - blk * d
        first = jnp.logical_and(i == 0, c == 0)
        valid = band & jnp.logical_or(jnp.logical_not(first), ki >= blk)
        for s in range(2):
            q = q_ref[s, rows(qs, blk), :].astype(BF16)
            kk = kbuf_ref[s, rows(ks, 2 * blk), :]
            vv = vbuf_ref[s, rows(ks, 2 * blk), :]
            o_acc = jnp.zeros((blk, LANES), F32)
            lse_acc = jnp.zeros((blk, LANES), F32)
            for h in range(2):
                hm = head == h
                sc = _nt_dot(q, jnp.where(hm, kk, 0.0).astype(BF16))
                sc = jnp.where(valid, sc, NEG)
                m = jnp.max(sc, axis=-1, keepdims=True)
                p = jnp.exp(sc - m)
                l = jnp.sum(p, axis=-1, keepdims=True)
                u = jnp.dot(p.astype(BF16), jnp.where(hm, vv, 0.0).astype(BF16),
                            preferred_element_type=F32)
                o_acc = o_acc + u * (1.0 / l)
                lse_acc = jnp.where(hm, m + jnp.log(l), lse_acc)
            o_ref[s, rows(qs, blk), :] = o_acc
            lse_ref[s, rows(qs, blk), :] = lse_acc
        return carry

    lax.fori_loop(0, tile // blk, block, 0, unroll=2)
    kbuf_ref[:, 0:tile, :] = kbuf_ref[:, tile:, :]
    vbuf_ref[:, 0:tile, :] = vbuf_ref[:, tile:, :]


def _dilated_group(g, d, q, k, v):
    _, B, _, S, _ = q.shape
    tile = DIL_TILE
    in_spec = pl.BlockSpec((None, None, 2, tile, LANES), lambda b, i: (g, b, 0, i, 0))
    out_spec = pl.BlockSpec((None, 2, tile, LANES), lambda b, i: (b, 0, i, 0))
    out = jax.ShapeDtypeStruct((B, 2, S, LANES), F32)
    return pl.pallas_call(
        functools.partial(_dilated_kernel, d),
        grid=(B, S // tile),
        in_specs=[in_spec, in_spec, in_spec],
        out_specs=[out_spec, out_spec],
        out_shape=[out, out],
        scratch_shapes=[pltpu.VMEM((2, 2 * tile, LANES), F32), pltpu.VMEM((2, 2 * tile, LANES), F32)],
        compiler_params=pltpu.CompilerParams(dimension_semantics=("arbitrary", "arbitrary"),
                                             vmem_limit_bytes=VMEM_LIMIT),
        name=f"dilated_d{d}",
    )(q, k, v)


def _layer_b_out_kernel(x_ref, o0_ref, l0_ref, o1_ref, l1_ref, o2_ref, l2_ref, qm_ref, kst_ref, vst_ref,
                        wout_ref, bout_ref, lng_ref, lnb_ref, wrt_ref, br_ref,
                        x1_ref, x1b_ref, gate_ref, pos_ref, cnt_ref, cat_ref):
    for s in range(2):
        lses = [l0_ref[s], l1_ref[s], l2_ref[s]]
        outs = [o0_ref[s], o1_ref[s], o2_ref[s]]
        m = jnp.maximum(jnp.maximum(lses[0], lses[1]), lses[2])
        es = [jnp.exp(l - m) for l in lses]
        inv = 1.0 / (es[0] + es[1] + es[2])
        dil = (es[0] * inv) * outs[0] + (es[1] * inv) * outs[1] + (es[2] * inv) * outs[2]
        cat_ref[:, s * LANES:(s + 1) * LANES] = dil.astype(BF16)
    cat_ref[:, DIL_WIDTH:] = _mem_attention(qm_ref[...], kst_ref, vst_ref).astype(BF16)
    y = jnp.dot(cat_ref[...], wout_ref[...], preferred_element_type=F32) + bout_ref[...]
    _residual_ln_router(x_ref[...], y, lng_ref, lnb_ref, wrt_ref, br_ref,
                        x1_ref, x1b_ref, gate_ref, pos_ref, cnt_ref)


def _layer_b_out(x, S, dil, qm, kst, vst, wout, bout, lng, lnb, wrt, br):
    T = x.shape[0]
    ts = TOKEN_TILE
    tpb = S // ts
    out_shapes, out_specs = _router_out(T, ts)
    kv_spec = pl.BlockSpec((None, 4, MEM_TOKENS, MEM_WIDTH), lambda i: (i // tpb, 0, 0, 0))
    slab_spec = pl.BlockSpec((None, 2, ts, LANES), lambda i: (i // tpb, 0, i % tpb, 0))
    b_out = DIL_WIDTH + MEM_WIDTH
    return pl.pallas_call(
        _layer_b_out_kernel,
        grid=(T // ts,),
        in_specs=[pl.BlockSpec((ts, D_MODEL), lambda i: (i, 0))] + [slab_spec] * 6 + [
            pl.BlockSpec((ts, MEM_WIDTH), lambda i: (i, 0)), kv_spec, kv_spec,
            _const_spec((b_out, D_MODEL)), _const_spec((1, D_MODEL)),
            _const_spec((1, D_MODEL)), _const_spec((1, D_MODEL)),
            _const_spec((N_EXPERTS, D_MODEL)), _const_spec((N_EXPERTS, 1))],
        out_specs=out_specs,
        out_shape=out_shapes,
        scratch_shapes=[pltpu.VMEM((ts, b_out), BF16)],
        compiler_params=pltpu.CompilerParams(dimension_semantics=("arbitrary",),
                                             vmem_limit_bytes=VMEM_LIMIT),
        name="layer_b_out",
    )(x, *dil, qm, kst, vst, wout, bout, lng, lnb, wrt, br)


def _used_block(i, nu):
    return jnp.maximum(jnp.minimum(i, nu[0] - 1), 0)


def _expert_kernel(be_ref, nused_ref, rows_ref, gate_ref, wup_ref, bup_ref, wdn_ref, bdn_ref, y_ref,
                   wup_bf_ref, wdn_bf_ref):
    i = pl.program_id(0)
    used = i < nused_ref[0]

    @pl.when(jnp.logical_not(used))
    def _():
        y_ref[...] = jnp.zeros(y_ref.shape, y_ref.dtype)

    b = _used_block(i, nused_ref)
    new_expert = jnp.logical_or(i == 0, be_ref[b] != be_ref[jnp.maximum(b - 1, 0)])

    @pl.when(jnp.logical_and(used, new_expert))
    def _():
        wup_bf_ref[...] = wup_ref[...].astype(BF16)
        wdn_bf_ref[...] = wdn_ref[...].astype(BF16)

    @pl.when(used)
    def _():
        sub = rows_ref.shape[0] // EXPERT_SPLIT
        for s in range(EXPERT_SPLIT):
            rs = slice(s * sub, (s + 1) * sub)
            h = jnp.dot(rows_ref[rs, :], wup_bf_ref[...], preferred_element_type=F32) + bup_ref[...]
            g = jnp.minimum(h[:, :D_FF], SWIGLU_LIMIT)
            lin = jnp.clip(h[:, D_FF:], -SWIGLU_LIMIT, SWIGLU_LIMIT)
            act = g * jax.nn.sigmoid(SWIGLU_ALPHA * g) * (lin + 1.0)
            y = jnp.dot(act.astype(BF16), wdn_bf_ref[...], preferred_element_type=F32) + bdn_ref[...]
            gate = gate_ref[rs, :]
            for c in range(D_MODEL // LANES):
                y_ref[rs, c * LANES:(c + 1) * LANES] = (
                    y[:, c * LANES:(c + 1) * LANES] * gate).astype(y_ref.dtype)


def _experts(layer, block_expert, n_used, rows, row_gate, wup, bup, wdn, bdn):
    nr = rows.shape[0]
    r = EXPERT_ROWS

    def blk(i, be, nu):
        return _used_block(i, nu)

    def wmap(i, be, nu):
        return (layer, be[blk(i, be, nu)], 0, 0)

    grid_spec = pltpu.PrefetchScalarGridSpec(
        num_scalar_prefetch=2,
        grid=(nr // r,),
        in_specs=[pl.BlockSpec((r, D_MODEL), lambda i, be, nu: (blk(i, be, nu), 0)),
                  pl.BlockSpec((r, LANES), lambda i, be, nu: (blk(i, be, nu), 0)),
                  pl.BlockSpec((None, None, D_MODEL, 2 * D_FF), wmap),
                  pl.BlockSpec((None, None, 1, 2 * D_FF), wmap),
                  pl.BlockSpec((None, None, D_FF, D_MODEL), wmap),
                  pl.BlockSpec((None, None, 1, D_MODEL), wmap)],
        out_specs=pl.BlockSpec((r, D_MODEL), lambda i, be, nu: (i, 0)),
        scratch_shapes=[pltpu.VMEM((D_MODEL, 2 * D_FF), BF16), pltpu.VMEM((D_FF, D_MODEL), BF16)],
    )
    return pl.pallas_call(
        _expert_kernel,
        grid_spec=grid_spec,
        out_shape=jax.ShapeDtypeStruct((nr, D_MODEL), BF16),
        compiler_params=pltpu.CompilerParams(dimension_semantics=("arbitrary",),
                                             vmem_limit_bytes=VMEM_LIMIT),
        name="experts",
    )(block_expert, n_used, rows, row_gate, wup, bup, wdn, bdn)


def _sort_trips(n_rows):
    step = SORT_ROWS * SORT_UNROLL
    return (n_rows + step - 1) // step


def _segment_copies(tile, len_ref, loc_ref, glob_ref, make_copies, act):
    def per_expert(e, carry):
        n = len_ref[tile * N_EXPERTS + e]
        loc = loc_ref[tile * N_EXPERTS + e]
        glob = glob_ref[tile * N_EXPERTS + e]
        off = jnp.int32(0)
        for bit in SEG_BITS:
            part = n & bit

            @pl.when(part != 0)
            def _():
                lo = pl.multiple_of(loc + off, SEG_ALIGN)
                go = pl.multiple_of(glob + off, SEG_ALIGN)
                for cp in make_copies(lo, go, bit):
                    act(cp)

            off = off + part
        return carry

    lax.fori_loop(0, N_EXPERTS, per_expert, 0)


def _dispatch_kernel(len_ref, loc_ref, glob_ref, tot_ref, nused_ref,
                     xb_ref, pos_ref, gate_ref,
                     rows_hbm, gates_hbm, xs_ref, gs_ref, sem):
    i = pl.program_id(0)
    nt = pl.num_programs(0)
    slot = i % 2
    ts = xb_ref.shape[0]

    def copies(s):
        def make(lo, go, n):
            return (pltpu.make_async_copy(xs_ref.at[s, pl.ds(lo, n)], rows_hbm.at[pl.ds(go, n)], sem.at[0, s]),
                    pltpu.make_async_copy(gs_ref.at[s, pl.ds(lo, n)], gates_hbm.at[pl.ds(go, n)], sem.at[1, s]))
        return make

    tables = (len_ref, loc_ref, glob_ref)

    @pl.when(i >= 2)
    def _():
        _segment_copies(i - 2, *tables, copies(slot), lambda cp: cp.wait())

    xb = xb_ref[...]

    def sort_rows(c, carry):
        for u in range(SORT_UNROLL):
            r0 = pl.multiple_of((c * SORT_UNROLL + u) * SORT_ROWS, SORT_ROWS)
            prow = lax.broadcasted_iota(jnp.int32, (SORT_ROWS, ts), 0) + r0
            hit = None
            wg = jnp.zeros((SORT_ROWS, ts), F32)
            for k in range(TOP_K):
                hk = prow == pos_ref[k:k + 1, :]
                hit = hk if hit is None else (hit | hk)
                wg = jnp.where(hk, gate_ref[k:k + 1, :], wg)
            onehot = jnp.where(hit, 1.0, 0.0).astype(BF16)
            xs_ref[slot, pl.ds(r0, SORT_ROWS), :] = jnp.dot(onehot, xb, preferred_element_type=F32).astype(BF16)
            g = jnp.sum(wg, axis=1, keepdims=True)
            gs_ref[slot, pl.ds(r0, SORT_ROWS), :] = jnp.broadcast_to(g, (SORT_ROWS, LANES))
        return carry

    lax.fori_loop(0, _sort_trips(tot_ref[i]), sort_rows, 0)

    @pl.when(i == nt - 1)
    def _():
        xs_ref[slot, 0:EXPERT_ROWS, :] = jnp.zeros((EXPERT_ROWS, D_MODEL), BF16)
        gs_ref[slot, 0:EXPERT_ROWS, :] = jnp.zeros((EXPERT_ROWS, LANES), F32)

    _segment_copies(i, *tables, copies(slot), lambda cp: cp.start())

    def unused_blocks(act):
        def per_block(j, carry):
            for cp in copies(slot)(0, pl.multiple_of(j * EXPERT_ROWS, EXPERT_ROWS), EXPERT_ROWS):
                act(cp)
            return carry
        lax.fori_loop(nused_ref[0], rows_hbm.shape[0] // EXPERT_ROWS, per_block, 0)

    @pl.when(i == nt - 1)
    def _():
        unused_blocks(lambda cp: cp.start())

        @pl.when(i >= 1)
        def _():
            _segment_copies(i - 1, *tables, copies(1 - slot), lambda cp: cp.wait())
        _segment_copies(i, *tables, copies(slot), lambda cp: cp.wait())
        unused_blocks(lambda cp: cp.wait())


def _dispatch(seg_len, seg_loc, seg_glob, tile_tot, n_used, x1b, pos_t, gate_t, n_rows):
    T = x1b.shape[0]
    ts = TOKEN_TILE
    nt = T // ts
    any_spec = pl.BlockSpec(memory_space=pl.ANY)

    def tile(i, *_):
        return jnp.minimum(i, nt - 1)

    grid_spec = pltpu.PrefetchScalarGridSpec(
        num_scalar_prefetch=5,
        grid=(nt + 1,),
        in_specs=[pl.BlockSpec((ts, D_MODEL), lambda i, *_: (tile(i), 0)),
                  pl.BlockSpec((TOP_K, ts), lambda i, *_: (0, tile(i))),
                  pl.BlockSpec((TOP_K, ts), lambda i, *_: (0, tile(i)))],
        out_specs=[any_spec, any_spec],
        scratch_shapes=[pltpu.VMEM((2, SORTED_ROWS, D_MODEL), BF16),
                        pltpu.VMEM((2, SORTED_ROWS, LANES), F32),
                        pltpu.SemaphoreType.DMA((2, 2))],
    )
    return pl.pallas_call(
        _dispatch_kernel,
        grid_spec=grid_spec,
        out_shape=[jax.ShapeDtypeStruct((n_rows, D_MODEL), BF16), jax.ShapeDtypeStruct((n_rows, LANES), F32)],
        compiler_params=pltpu.CompilerParams(dimension_semantics=("arbitrary",),
                                             vmem_limit_bytes=VMEM_LIMIT),
        name="dispatch",
    )(seg_len, seg_loc, seg_glob, tile_tot, n_used, x1b, pos_t, gate_t)


def _combine_kernel(len_ref, loc_ref, glob_ref, tot_ref,
                    pos_ref, x1_ref, lng_ref, lnb_ref, y_hbm, o_ref, ybuf_ref, posb_ref, acc_ref, sem):
    i = pl.program_id(0)
    nt = pl.num_programs(0)
    slot = i % 2
    ts = x1_ref.shape[0]

    def copies(s):
        def make(lo, go, n):
            return (pltpu.make_async_copy(y_hbm.at[pl.ds(go, n)], ybuf_ref.at[s, pl.ds(lo, n)], sem.at[s]),)
        return make

    tables = (len_ref, loc_ref, glob_ref)

    @pl.when(i == 0)
    def _():
        ybuf_ref[...] = jnp.zeros(ybuf_ref.shape, ybuf_ref.dtype)
        _segment_copies(0, *tables, copies(0), lambda cp: cp.start())

    @pl.when(i + 1 < nt)
    def _():
        _segment_copies(i + 1, *tables, copies(1 - slot), lambda cp: cp.start())

    for k in range(TOP_K):
        posb_ref[k] = jnp.broadcast_to(pos_ref[:, k:k + 1], (ts, LANES))
    acc_ref[...] = jnp.zeros((ts, D_MODEL), F32)
    _segment_copies(i, *tables, copies(slot), lambda cp: cp.wait())
    lane = lax.broadcasted_iota(jnp.int32, (ts, LANES), 1)

    def gather_rows(c, carry):
        for u in range(SORT_UNROLL):
            r0 = pl.multiple_of((c * SORT_UNROLL + u) * SORT_ROWS, SORT_ROWS)
            halves = []
            for half in range(SORT_ROWS // LANES):
                col = lane + (r0 + half * LANES)
                hit = None
                for k in range(TOP_K):
                    hk = posb_ref[k] == col
                    hit = hk if hit is None else (hit | hk)
                halves.append(jnp.where(hit, 1.0, 0.0).astype(BF16))
            onehot = jnp.concatenate(halves, axis=1)
            acc_ref[...] += jnp.dot(onehot, ybuf_ref[slot, pl.ds(r0, SORT_ROWS), :],
                                    preferred_element_type=F32)
        return carry

    lax.fori_loop(0, _sort_trips(tot_ref[i]), gather_rows, 0)
    o_ref[...] = _layer_norm(DEEPNORM_ALPHA * x1_ref[...] + acc_ref[...], lng_ref[...], lnb_ref[...])


def _combine(seg_len, seg_loc, seg_glob, tile_tot, pos, x1, lng, lnb, y):
    T = x1.shape[0]
    ts = TOKEN_TILE
    grid_spec = pltpu.PrefetchScalarGridSpec(
        num_scalar_prefetch=4,
        grid=(T // ts,),
        in_specs=[pl.BlockSpec((ts, TOP_K), lambda i, *_: (i, 0)),
                  pl.BlockSpec((ts, D_MODEL), lambda i, *_: (i, 0)),
                  pl.BlockSpec((1, D_MODEL), lambda i, *_: (0, 0)),
                  pl.BlockSpec((1, D_MODEL), lambda i, *_: (0, 0)),
                  pl.BlockSpec(memory_space=pl.ANY)],
        out_specs=pl.BlockSpec((ts, D_MODEL), lambda i, *_: (i, 0)),
        scratch_shapes=[pltpu.VMEM((2, SORTED_ROWS, D_MODEL), BF16),
                        pltpu.VMEM((TOP_K, ts, LANES), jnp.int32),
                        pltpu.VMEM((ts, D_MODEL), F32),
                        pltpu.SemaphoreType.DMA((2,))],
    )
    return pl.pallas_call(
        _combine_kernel,
        grid_spec=grid_spec,
        out_shape=jax.ShapeDtypeStruct((T, D_MODEL), F32),
        compiler_params=pltpu.CompilerParams(dimension_semantics=("arbitrary",),
                                             vmem_limit_bytes=VMEM_LIMIT),
        name="combine",
    )(seg_len, seg_loc, seg_glob, tile_tot, pos, x1, lng, lnb, y)


def _moe(layer, x1, x1b, gate_t, pos_t, cnt, wup, bup, wdn, bdn, lng, lnb):
    T = x1.shape[0]
    ts = TOKEN_TILE
    nt = T // ts
    r = EXPERT_ROWS
    n_blocks = -(-(T * TOP_K + nt * N_EXPERTS * (SEG_ALIGN - 1) + N_EXPERTS * (r - 1)) // r)
    counts = cnt[:, 0].reshape(nt, N_EXPERTS).astype(jnp.int32)
    seg_len = (counts + SEG_ALIGN - 1) // SEG_ALIGN * SEG_ALIGN
    seg_loc = jnp.cumsum(seg_len, axis=1) - seg_len
    tile_tot = jnp.sum(seg_len, axis=1)
    region = jnp.sum(seg_len, axis=0)
    region = (region + r - 1) // r * r
    region_end = jnp.cumsum(region)
    seg_glob = (region_end - region)[None, :] + jnp.cumsum(seg_len, axis=0) - seg_len
    n_used = (region_end[-1:] // r).astype(jnp.int32)
    block_start = jnp.arange(n_blocks, dtype=jnp.int32) * r
    block_expert = jnp.minimum(jnp.sum(region_end[None, :] <= block_start[:, None], axis=1),
                               N_EXPERTS - 1).astype(jnp.int32)
    used = jnp.sum(seg_len, axis=0)
    zero_row = jnp.zeros((1, N_EXPERTS), jnp.int32)
    tables = (jnp.concatenate([seg_len, (region - used)[None, :]]).reshape(-1),
              jnp.concatenate([seg_loc, zero_row]).reshape(-1),
              jnp.concatenate([seg_glob, (region_end - region + used)[None, :]]).reshape(-1),
              jnp.concatenate([tile_tot, jnp.zeros((1,), jnp.int32)]))
    rows, row_gate = _dispatch(*tables, n_used, x1b, pos_t, gate_t, n_blocks * r)
    y = _experts(layer, block_expert, n_used, rows, row_gate, wup, bup, wdn, bdn)
    return _combine(*tables, pos_t.T, x1, lng, lnb, y)


def kernel(x, mem, positions, a_w_in, a_b_in, a_dw, a_dw_b, a_cn_g, a_cn_b, a_w_out, a_b_out, w_kv_shared, b_w_q, b_w_out, b_b_out, mem_w_kv, ln_g, ln_b, router_w, router_b, exp_w_up, exp_b_up, exp_w_down, exp_b_down):
    B, S, D = x.shape
    T = B * S
    assert D == D_MODEL and S % DIL_TILE == 0 and S % TOKEN_TILE == 0
    xt = x.reshape(T, D)
    kst, vst = _mem_kv(mem.astype(BF16), mem_w_kv.astype(BF16))
    pos = positions.reshape(T, 1).astype(jnp.int32)
    half = ROT_DIM // 2
    inv_freq = jnp.power(ROPE_THETA, -jnp.arange(half, dtype=F32) / half)
    e = jnp.arange(LANES) % HEAD_DIM
    freq = jnp.where(e < ROT_DIM, inv_freq[e % half], 0.0).reshape(1, LANES).astype(F32)
    dw_pad = jnp.pad(a_dw, ((0, 0), (0, 32 - CONV_WIDTH), (0, 0)))
    b_up = exp_b_up.reshape(DEPTH, N_EXPERTS, 1, 2 * D_FF)
    b_down = exp_b_down.reshape(DEPTH, N_EXPERTS, 1, D_MODEL)
    shared_kv = None
    for l in range(DEPTH):
        wrt = router_w[l].T
        br = router_b[l].reshape(N_EXPERTS, 1)
        lng0, lnb0 = _row(ln_g[l, 0]), _row(ln_b[l, 0])
        if l < N_A_LAYERS:
            outs = _layer_a(xt, S, a_w_in[l].astype(BF16), _row(a_b_in[l]), dw_pad[l], _row(a_dw_b[l]),
                            _row(a_cn_g[l]), _row(a_cn_b[l]), kst[l], vst[l],
                            a_w_out[l].astype(BF16), _row(a_b_out[l]), lng0, lnb0, wrt, br)
        else:
            j = l - N_A_LAYERS
            if j == 0:
                q, qm, k, v = _layer_b_proj(xt, B, S, pos, freq, b_w_q[j].astype(BF16), w_kv_shared.astype(BF16))
                shared_kv = (k, v)
            else:
                q, qm = _layer_b_proj(xt, B, S, pos, freq, b_w_q[j].astype(BF16), None)
            dil = []
            for g, (window, dilation) in enumerate(DIL_GROUPS):
                assert window // dilation == DIL_BLOCK and DIL_TILE % (DIL_BLOCK * dilation) == 0
                dil += _dilated_group(g, dilation, q, shared_kv[0], shared_kv[1])
            outs = _layer_b_out(xt, S, dil, qm, kst[l], vst[l], b_w_out[j].astype(BF16), _row(b_b_out[j]),
                                lng0, lnb0, wrt, br)
        x1, x1b, gate_t, pos_t, cnt = outs
        xt = _moe(l, x1, x1b, gate_t, pos_t, cnt, exp_w_up, b_up, exp_w_down, b_down,
                  _row(ln_g[l, 1]), _row(ln_b[l, 1]))
    return xt.reshape(B, S, D)
```

```python
import functools

import jax
import jax.numpy as jnp
from jax import lax
from jax.experimental import pallas as pl
from jax.experimental.pallas import tpu as pltpu

D_MODEL = 1024
DEPTH = 4
N_A_LAYERS = 2
HEAD_DIM = 64
ROT_DIM = 16
ROPE_THETA = 500000.0
CONV_CH = 768
CONV_WIDTH = 31
MEM_TOKENS = 256
MEM_WIDTH = 256
DIL_GROUPS = ((128, 1), (512, 4), (2048, 16))
DIL_WIDTH = 256
DIL_BLOCK = 128
N_EXPERTS = 32
TOP_K = 4
D_FF = 1024
SWIGLU_ALPHA = 1.702
SWIGLU_LIMIT = 7.0
DEEPNORM_ALPHA = (2 * DEPTH) ** 0.25
LN_EPS = 1e-5
ATTN_SCALE = HEAD_DIM ** -0.5

LANES = 128
TOKEN_TILE = 512
DIL_TILE = 2048
EXPERT_ROWS = 512
EXPERT_SPLIT = 1
CONV_ROWS = 64
CONV_LANES = 384
SEG_ALIGN = 16
SEG_BITS = tuple(SEG_ALIGN << b for b in reversed(range(6)))
SORT_ROWS = 256
SORT_UNROLL = 2
SORTED_ROWS = 2560
SORT_GROUPS = SORTED_ROWS // SEG_ALIGN
VMEM_LIMIT = 56 * 1024 * 1024

F32 = jnp.float32
BF16 = jnp.bfloat16
NEG = -1e30


def _layer_norm(v, g, b):
    mu = jnp.mean(v, axis=-1, keepdims=True)
    c = v - mu
    var = jnp.mean(c * c, axis=-1, keepdims=True)
    return c * lax.rsqrt(var + LN_EPS) * g + b


def _nt_dot(a, b, **kw):
    return lax.dot_general(a, b, (((1,), (1,)), ((), ())), preferred_element_type=F32, **kw)


def _mem_attention(qm_bf16, kst_ref, vst_ref):
    out = None
    for h in range(4):
        s = _nt_dot(qm_bf16, kst_ref[h])
        m = jnp.max(s, axis=-1, keepdims=True)
        p = jnp.exp(s - m)
        l = jnp.sum(p, axis=-1, keepdims=True)
        pn = (p * (1.0 / l)).astype(BF16)
        u = jnp.dot(pn, vst_ref[h], preferred_element_type=F32)
        out = u if out is None else out + u
    return out


def _residual_ln_router(x_res, y, lng_ref, lnb_ref, wrt_ref, br_ref,
                        x1_ref, x1b_ref, pos_ref, pose_ref, gatee_ref, cnt_ref):
    ts = x_res.shape[0]
    x1 = _layer_norm(DEEPNORM_ALPHA * x_res + y, lng_ref[...], lnb_ref[...])
    x1_ref[...] = x1
    x1b_ref[...] = x1.astype(BF16)
    logit = _nt_dot(wrt_ref[...], x1, precision=lax.Precision.HIGHEST) + br_ref[...]
    row = lax.broadcasted_iota(jnp.int32, (N_EXPERTS, ts), 0)
    vals, onehots = [], []
    for k in range(TOP_K):
        m = jnp.max(logit, axis=0, keepdims=True)
        idx = jnp.min(jnp.where(logit == m, row, N_EXPERTS), axis=0, keepdims=True)
        oh = row == idx
        logit = jnp.where(oh, -jnp.inf, logit)
        vals.append(m)
        onehots.append(oh)
    exps = [jnp.exp(v - vals[0]) for v in vals]
    inv = 1.0 / (exps[0] + exps[1] + exps[2] + exps[3])
    gate_e = jnp.zeros((N_EXPERTS, ts), F32)
    for k in range(TOP_K):
        gate_e = jnp.where(onehots[k], exps[k] * inv, gate_e)
    gatee_ref[...] = gate_e
    oh_all = (onehots[0] | onehots[1] | onehots[2] | onehots[3])
    oh_f = jnp.where(oh_all, 1.0, 0.0)
    ri = lax.broadcasted_iota(jnp.int32, (ts, ts), 0)
    ci = lax.broadcasted_iota(jnp.int32, (ts, ts), 1)
    tri = jnp.where(ri < ci, 1.0, 0.0).astype(BF16)
    cum = jnp.dot(oh_f.astype(BF16), tri, preferred_element_type=F32)
    cnt = jnp.sum(oh_f, axis=1, keepdims=True)
    cnt_b = jnp.broadcast_to(cnt, (N_EXPERTS, LANES))
    cnt_ref[...] = cnt_b
    units = jnp.floor((cnt_b + (SEG_ALIGN - 1)) * (1.0 / SEG_ALIGN))
    er = lax.broadcasted_iota(jnp.int32, (N_EXPERTS, N_EXPERTS), 0)
    ec = lax.broadcasted_iota(jnp.int32, (N_EXPERTS, N_EXPERTS), 1)
    low_tri = jnp.where(ec < er, 1.0, 0.0).astype(BF16)
    seg_start = jnp.dot(low_tri, units.astype(BF16), preferred_element_type=F32)[:, 0:1] * SEG_ALIGN
    pos_all = cum + seg_start
    pose_ref[...] = jnp.where(oh_all, pos_all, -1.0).astype(jnp.int32)
    for k in range(TOP_K):
        rk = jnp.sum(jnp.where(onehots[k], pos_all, 0.0), axis=0, keepdims=True)
        pos_ref[k:k + 1, :] = rk.astype(jnp.int32)


def _mem_kv_kernel(mem_ref, w_ref, kst_ref, vst_ref):
    kv = jnp.dot(mem_ref[...], w_ref[...], preferred_element_type=F32)
    k = kv[:, :MEM_WIDTH] * ATTN_SCALE
    v = kv[:, MEM_WIDTH:]
    head = lax.broadcasted_iota(jnp.int32, (MEM_TOKENS, MEM_WIDTH), 1) // HEAD_DIM
    for h in range(4):
        kst_ref[h] = jnp.where(head == h, k, 0.0).astype(BF16)
        vst_ref[h] = jnp.where(head == h, v, 0.0).astype(BF16)


def _mem_kv(mem_b, w_b):
    B = mem_b.shape[0]
    out = jax.ShapeDtypeStruct((DEPTH, B, 4, MEM_TOKENS, MEM_WIDTH), BF16)
    return pl.pallas_call(
        _mem_kv_kernel,
        grid=(DEPTH, B),
        in_specs=[pl.BlockSpec((None, MEM_TOKENS, D_MODEL), lambda l, b: (b, 0, 0)),
                  pl.BlockSpec((None, D_MODEL, 2 * MEM_WIDTH), lambda l, b: (l, 0, 0))],
        out_specs=[pl.BlockSpec((None, None, 4, MEM_TOKENS, MEM_WIDTH), lambda l, b: (l, b, 0, 0, 0))] * 2,
        out_shape=[out, out],
        name="mem_kv",
    )(mem_b, w_b)


def _layer_a_kernel(tiles_per_batch,
                    x_ref, win_ref, bin_ref, dw_ref, dwb_ref, cng_ref, cnb_ref, kst_ref, vst_ref,
                    wout_ref, bout_ref, lng_ref, lnb_ref, wrt_ref, br_ref,
                    x1_ref, x1b_ref, pos_ref, pose_ref, gatee_ref, cnt_ref,
                    hpad_ref, shift_ref, cat_ref):
    ts = x_ref.shape[0]
    i = pl.program_id(0)
    x = x_ref[...]
    h = jnp.dot(x.astype(BF16), win_ref[...], preferred_element_type=F32) + bin_ref[...]
    hg = h[:, :CONV_CH] * jax.nn.sigmoid(h[:, CONV_CH:2 * CONV_CH])
    qm = h[:, 2 * CONV_CH:].astype(BF16)

    @pl.when(i % tiles_per_batch == 0)
    def _():
        hpad_ref[0:32, :] = jnp.zeros((32, CONV_CH), F32)

    hpad_ref[32:32 + ts, :] = hg
    for b in range(8):
        n = ts + 8 * ((CONV_WIDTH - 1 - b) // 8)
        for r0 in range(0, n, 128):
            rows = min(128, n - r0)
            shift_ref[b, r0:r0 + rows, :] = hpad_ref[pl.ds(2 + b + r0, rows), :]
    hpad_ref[0:32, :] = hpad_ref[ts:ts + 32, :]

    for c0 in range(0, CONV_CH, CONV_LANES):
        lanes = slice(c0, c0 + CONV_LANES)

        def conv_rows(c, carry, lanes=lanes):
            r0 = pl.multiple_of(c * CONV_ROWS, CONV_ROWS)
            acc = jnp.zeros((CONV_ROWS, CONV_LANES), F32) + dwb_ref[:, lanes]
            for k in range(CONV_WIDTH):
                a, b = divmod(k, 8)
                acc = acc + shift_ref[b, pl.ds(r0 + 8 * a, CONV_ROWS), lanes] * dw_ref[k:k + 1, lanes]
            hpad_ref[pl.ds(32 + r0, CONV_ROWS), lanes] = acc
            return carry

        lax.fori_loop(0, ts // CONV_ROWS, conv_rows, 0)

    def norm_rows(c, carry):
        r0 = pl.multiple_of(c * CONV_ROWS, CONV_ROWS)
        cn = _layer_norm(hpad_ref[pl.ds(32 + r0, CONV_ROWS), :], cng_ref[...], cnb_ref[...])
        cat_ref[pl.ds(r0, CONV_ROWS), 0:CONV_CH] = (cn * jax.nn.sigmoid(cn)).astype(BF16)
        return carry

    lax.fori_loop(0, ts // CONV_ROWS, norm_rows, 0)

    cat_ref[:, CONV_CH:] = _mem_attention(qm, kst_ref, vst_ref).astype(BF16)
    y = jnp.dot(cat_ref[...], wout_ref[...], preferred_element_type=F32) + bout_ref[...]
    _residual_ln_router(x, y, lng_ref, lnb_ref, wrt_ref, br_ref,
                        x1_ref, x1b_ref, pos_ref, pose_ref, gatee_ref, cnt_ref)


def _row(v):
    return v.reshape(1, -1).astype(F32)


def _const_spec(shape):
    nd = len(shape)
    return pl.BlockSpec(shape, lambda i: (0,) * nd)


def _router_out(T, ts):
    nt = T // ts
    shapes = [jax.ShapeDtypeStruct((T, D_MODEL), F32), jax.ShapeDtypeStruct((T, D_MODEL), BF16),
              jax.ShapeDtypeStruct((TOP_K, T), jnp.int32), jax.ShapeDtypeStruct((N_EXPERTS, T), jnp.int32),
              jax.ShapeDtypeStruct((N_EXPERTS, T), F32), jax.ShapeDtypeStruct((nt * N_EXPERTS, LANES), F32)]
    specs = [pl.BlockSpec((ts, D_MODEL), lambda i: (i, 0)), pl.BlockSpec((ts, D_MODEL), lambda i: (i, 0)),
             pl.BlockSpec((TOP_K, ts), lambda i: (0, i)), pl.BlockSpec((N_EXPERTS, ts), lambda i: (0, i)),
             pl.BlockSpec((N_EXPERTS, ts), lambda i: (0, i)), pl.BlockSpec((N_EXPERTS, LANES), lambda i: (i, 0))]
    return shapes, specs


def _layer_a(x, S, win, bin_, dw, dwb, cng, cnb, kst, vst, wout, bout, lng, lnb, wrt, br):
    T = x.shape[0]
    ts = TOKEN_TILE
    tpb = S // ts
    out_shapes, out_specs = _router_out(T, ts)
    kv_spec = pl.BlockSpec((None, 4, MEM_TOKENS, MEM_WIDTH), lambda i: (i // tpb, 0, 0, 0))
    a_in = 2 * CONV_CH + MEM_WIDTH
    return pl.pallas_call(
        functools.partial(_layer_a_kernel, tpb),
        grid=(T // ts,),
        in_specs=[pl.BlockSpec((ts, D_MODEL), lambda i: (i, 0)),
                  _const_spec((D_MODEL, a_in)), _const_spec((1, a_in)),
                  _const_spec((32, CONV_CH)), _const_spec((1, CONV_CH)),
                  _const_spec((1, CONV_CH)), _const_spec((1, CONV_CH)),
                  kv_spec, kv_spec,
                  _const_spec((D_MODEL, D_MODEL)), _const_spec((1, D_MODEL)),
                  _const_spec((1, D_MODEL)), _const_spec((1, D_MODEL)),
                  _const_spec((N_EXPERTS, D_MODEL)), _const_spec((N_EXPERTS, 1))],
        out_specs=out_specs,
        out_shape=out_shapes,
        scratch_shapes=[pltpu.VMEM((ts + 32, CONV_CH), F32),
                        pltpu.VMEM((8, ts + 24, CONV_CH), F32),
                        pltpu.VMEM((ts, D_MODEL), BF16)],
        compiler_params=pltpu.CompilerParams(dimension_semantics=("arbitrary",),
                                             vmem_limit_bytes=VMEM_LIMIT),
        name="layer_a",
    )(x, win, bin_, dw, dwb, cng, cnb, kst, vst, wout, bout, lng, lnb, wrt, br)


def _rotary(v, cos_t, sin_t, low):
    partner = jnp.where(low, pltpu.roll(v, LANES - 8, axis=1), pltpu.roll(v, 8, axis=1))
    return v * cos_t + partner * sin_t


def _rope_tables(pos_ref, freq_ref):
    e = lax.broadcasted_iota(jnp.int32, (1, LANES), 1) % HEAD_DIM
    ang = pos_ref[...].astype(F32) * freq_ref[...]
    low = e < ROT_DIM // 2
    rot = e < ROT_DIM
    cos_t = jnp.where(rot, jnp.cos(ang), 1.0)
    sn = jnp.sin(ang)
    sin_t = jnp.where(low, -sn, jnp.where(rot, sn, 0.0))
    return cos_t, sin_t, low


def _layer_b_proj_kernel(with_kv, x_ref, pos_ref, freq_ref, wq_ref, *rest):
    if with_kv:
        wkv_ref, q_ref, qm_ref, k_ref, v_ref = rest
    else:
        q_ref, qm_ref = rest
    xb = x_ref[...].astype(BF16)
    cos_t, sin_t, low = _rope_tables(pos_ref, freq_ref)
    q = jnp.dot(xb, wq_ref[...], preferred_element_type=F32)
    for g in range(3):
        for s in range(2):
            c0 = g * DIL_WIDTH + s * LANES
            q_ref[g, s] = _rotary(q[:, c0:c0 + LANES], cos_t, sin_t, low) * ATTN_SCALE
    qm_ref[...] = q[:, 3 * DIL_WIDTH:].astype(BF16)
    if with_kv:
        kv = jnp.dot(xb, wkv_ref[...], preferred_element_type=F32)
        for g in range(3):
            for s in range(2):
                c0 = g * 2 * DIL_WIDTH + s * LANES
                k_ref[g, s] = _rotary(kv[:, c0:c0 + LANES], cos_t, sin_t, low)
                v_ref[g, s] = kv[:, c0 + DIL_WIDTH:c0 + DIL_WIDTH + LANES]


def _layer_b_proj(x, B, S, pos, freq, wq, wkv):
    T = x.shape[0]
    ts = TOKEN_TILE
    tpb = S // ts
    with_kv = wkv is not None
    slab = jax.ShapeDtypeStruct((3, B, 2, S, LANES), F32)
    slab_spec = pl.BlockSpec((3, None, 2, ts, LANES), lambda i: (0, i // tpb, 0, i % tpb, 0))
    in_specs = [pl.BlockSpec((ts, D_MODEL), lambda i: (i, 0)),
                pl.BlockSpec((ts, 1), lambda i: (i, 0)),
                _const_spec((1, LANES)),
                _const_spec((D_MODEL, D_MODEL))]
    args = [x, pos, freq, wq]
    out_shape = [slab, jax.ShapeDtypeStruct((T, MEM_WIDTH), BF16)]
    out_specs = [slab_spec, pl.BlockSpec((ts, MEM_WIDTH), lambda i: (i, 0))]
    if with_kv:
        in_specs.append(_const_spec((D_MODEL, 6 * DIL_WIDTH)))
        args.append(wkv)
        out_shape += [slab, slab]
        out_specs += [slab_spec, slab_spec]
    return pl.pallas_call(
        functools.partial(_layer_b_proj_kernel, with_kv),
        grid=(T // ts,),
        in_specs=in_specs, out_specs=out_specs, out_shape=out_shape,
        compiler_params=pltpu.CompilerParams(dimension_semantics=("arbitrary",),
                                             vmem_limit_bytes=VMEM_LIMIT),
        name="layer_b_proj_kv" if with_kv else "layer_b_proj",
    )(*args)


def _dilated_kernel(d, q_ref, k_ref, v_ref, o_ref, lse_ref, kbuf_ref, vbuf_ref):
    i = pl.program_id(1)
    tile = DIL_TILE
    blk = DIL_BLOCK

    @pl.when(i == 0)
    def _():
        kbuf_ref[:, 0:tile, :] = jnp.zeros((2, tile, LANES), F32)
        vbuf_ref[:, 0:tile, :] = jnp.zeros((2, tile, LANES), F32)

    kbuf_ref[:, tile:, :] = k_ref[...]
    vbuf_ref[:, tile:, :] = v_ref[...]

    def rows(start, n):
        return pl.ds(start, n) if d == 1 else pl.ds(start, n, stride=d)

    qi = lax.broadcasted_iota(jnp.int32, (blk, 2 * blk), 0)
    ki = lax.broadcasted_iota(jnp.int32, (blk, 2 * blk), 1)
    dist = qi + blk - ki
    band = (dist >= 0) & (dist <= blk)
    head = lax.broadcasted_iota(jnp.int32, (1, LANES), 1) // HEAD_DIM

    def block(j, carry):
        c = j // d
        r = j % d
        qs = c * (blk * d) + r
        ks = tile + qs - blk * d
        first = jnp.logical_and(i == 0, c == 0)
        valid = band & jnp.logical_or(jnp.logical_not(first), ki >= blk)
        for s in range(2):
            q = q_ref[s, rows(qs, blk), :].astype(BF16)
            kk = kbuf_ref[s, rows(ks, 2 * blk), :]
            vv = vbuf_ref[s, rows(ks, 2 * blk), :]
            o_acc = jnp.zeros((blk, LANES), F32)
            lse_acc = jnp.zeros((blk, LANES), F32)
            for h in range(2):
                hm = head == h
                sc = _nt_dot(q, jnp.where(hm, kk, 0.0).astype(BF16))
                sc = jnp.where(valid, sc, NEG)
                m = jnp.max(sc, axis=-1, keepdims=True)
                p = jnp.exp(sc - m)
                l = jnp.sum(p, axis=-1, keepdims=True)
                u = jnp.dot(p.astype(BF16), jnp.where(hm, vv, 0.0).astype(BF16),
                            preferred_element_type=F32)
                o_acc = o_acc + u * (1.0 / l)
                lse_acc = jnp.where(hm, m + jnp.log(l), lse_acc)
            o_ref[s, rows(qs, blk), :] = o_acc
            lse_ref[s, rows(qs, blk), :] = lse_acc
        return carry

    lax.fori_loop(0, tile // blk, block, 0, unroll=4)
    kbuf_ref[:, 0:tile, :] = kbuf_ref[:, tile:, :]
    vbuf_ref[:, 0:tile, :] = vbuf_ref[:, tile:, :]


def _dilated_group(g, d, q, k, v):
    _, B, _, S, _ = q.shape
    tile = DIL_TILE
    in_spec = pl.BlockSpec((None, None, 2, tile, LANES), lambda b, i: (g, b, 0, i, 0))
    out_spec = pl.BlockSpec((None, 2, tile, LANES), lambda b, i: (b, 0, i, 0))
    out = jax.ShapeDtypeStruct((B, 2, S, LANES), F32)
    return pl.pallas_call(
        functools.partial(_dilated_kernel, d),
        grid=(B, S // tile),
        in_specs=[in_spec, in_spec, in_spec],
        out_specs=[out_spec, out_spec],
        out_shape=[out, out],
        scratch_shapes=[pltpu.VMEM((2, 2 * tile, LANES), F32), pltpu.VMEM((2, 2 * tile, LANES), F32)],
        compiler_params=pltpu.CompilerParams(dimension_semantics=("arbitrary", "arbitrary"),
                                             vmem_limit_bytes=VMEM_LIMIT),
        name=f"dilated_d{d}",
    )(q, k, v)


def _layer_b_out_kernel(x_ref, o0_ref, l0_ref, o1_ref, l1_ref, o2_ref, l2_ref, qm_ref, kst_ref, vst_ref,
                        wout_ref, bout_ref, lng_ref, lnb_ref, wrt_ref, br_ref,
                        x1_ref, x1b_ref, pos_ref, pose_ref, gatee_ref, cnt_ref, cat_ref):
    for s in range(2):
        lses = [l0_ref[s], l1_ref[s], l2_ref[s]]
        outs = [o0_ref[s], o1_ref[s], o2_ref[s]]
        m = jnp.maximum(jnp.maximum(lses[0], lses[1]), lses[2])
        es = [jnp.exp(l - m) for l in lses]
        inv = 1.0 / (es[0] + es[1] + es[2])
        dil = (es[0] * inv) * outs[0] + (es[1] * inv) * outs[1] + (es[2] * inv) * outs[2]
        cat_ref[:, s * LANES:(s + 1) * LANES] = dil.astype(BF16)
    cat_ref[:, DIL_WIDTH:] = _mem_attention(qm_ref[...], kst_ref, vst_ref).astype(BF16)
    y = jnp.dot(cat_ref[...], wout_ref[...], preferred_element_type=F32) + bout_ref[...]
    _residual_ln_router(x_ref[...], y, lng_ref, lnb_ref, wrt_ref, br_ref,
                        x1_ref, x1b_ref, pos_ref, pose_ref, gatee_ref, cnt_ref)


def _layer_b_out(x, S, dil, qm, kst, vst, wout, bout, lng, lnb, wrt, br):
    T = x.shape[0]
    ts = TOKEN_TILE
    tpb = S // ts
    out_shapes, out_specs = _router_out(T, ts)
    kv_spec = pl.BlockSpec((None, 4, MEM_TOKENS, MEM_WIDTH), lambda i: (i // tpb, 0, 0, 0))
    slab_spec = pl.BlockSpec((None, 2, ts, LANES), lambda i: (i // tpb, 0, i % tpb, 0))
    b_out = DIL_WIDTH + MEM_WIDTH
    return pl.pallas_call(
        _layer_b_out_kernel,
        grid=(T // ts,),
        in_specs=[pl.BlockSpec((ts, D_MODEL), lambda i: (i, 0))] + [slab_spec] * 6 + [
            pl.BlockSpec((ts, MEM_WIDTH), lambda i: (i, 0)), kv_spec, kv_spec,
            _const_spec((b_out, D_MODEL)), _const_spec((1, D_MODEL)),
            _const_spec((1, D_MODEL)), _const_spec((1, D_MODEL)),
            _const_spec((N_EXPERTS, D_MODEL)), _const_spec((N_EXPERTS, 1))],
        out_specs=out_specs,
        out_shape=out_shapes,
        scratch_shapes=[pltpu.VMEM((ts, b_out), BF16)],
        compiler_params=pltpu.CompilerParams(dimension_semantics=("arbitrary",),
                                             vmem_limit_bytes=VMEM_LIMIT),
        name="layer_b_out",
    )(x, *dil, qm, kst, vst, wout, bout, lng, lnb, wrt, br)


def _used_block(i, nu):
    return jnp.maximum(jnp.minimum(i, nu[0] - 1), 0)


def _expert_kernel(be_ref, nused_ref, rows_ref, gate_ref, wup_ref, bup_ref, wdn_ref, bdn_ref, y_ref,
                   wup_bf_ref, wdn_bf_ref):
    i = pl.program_id(0)
    used = i < nused_ref[0]

    @pl.when(jnp.logical_not(used))
    def _():
        y_ref[...] = jnp.zeros(y_ref.shape, y_ref.dtype)

    b = _used_block(i, nused_ref)
    new_expert = jnp.logical_or(i == 0, be_ref[b] != be_ref[jnp.maximum(b - 1, 0)])

    @pl.when(jnp.logical_and(used, new_expert))
    def _():
        wup_bf_ref[...] = wup_ref[...].astype(BF16)
        wdn_bf_ref[...] = wdn_ref[...].astype(BF16)

    @pl.when(used)
    def _():
        sub = rows_ref.shape[0] // EXPERT_SPLIT
        for s in range(EXPERT_SPLIT):
            rs = slice(s * sub, (s + 1) * sub)
            h = jnp.dot(rows_ref[rs, :], wup_bf_ref[...], preferred_element_type=F32) + bup_ref[...]
            g = jnp.minimum(h[:, :D_FF], SWIGLU_LIMIT)
            lin = jnp.clip(h[:, D_FF:], -SWIGLU_LIMIT, SWIGLU_LIMIT)
            act = g * jax.nn.sigmoid(SWIGLU_ALPHA * g) * (lin + 1.0)
            y = jnp.dot(act.astype(BF16), wdn_bf_ref[...], preferred_element_type=F32) + bdn_ref[...]
            gate = gate_ref[rs, :]
            for c in range(D_MODEL // LANES):
                y_ref[rs, c * LANES:(c + 1) * LANES] = (
                    y[:, c * LANES:(c + 1) * LANES] * gate).astype(y_ref.dtype)


def _experts(layer, block_expert, n_used, rows, row_gate, wup, bup, wdn, bdn):
    nr = rows.shape[0]
    r = EXPERT_ROWS

    def blk(i, be, nu):
        return _used_block(i, nu)

    def wmap(i, be, nu):
        return (layer, be[blk(i, be, nu)], 0, 0)

    grid_spec = pltpu.PrefetchScalarGridSpec(
        num_scalar_prefetch=2,
        grid=(nr // r,),
        in_specs=[pl.BlockSpec((r, D_MODEL), lambda i, be, nu: (blk(i, be, nu), 0)),
                  pl.BlockSpec((r, LANES), lambda i, be, nu: (blk(i, be, nu), 0)),
                  pl.BlockSpec((None, None, D_MODEL, 2 * D_FF), wmap),
                  pl.BlockSpec((None, None, 1, 2 * D_FF), wmap),
                  pl.BlockSpec((None, None, D_FF, D_MODEL), wmap),
                  pl.BlockSpec((None, None, 1, D_MODEL), wmap)],
        out_specs=pl.BlockSpec((r, D_MODEL), lambda i, be, nu: (i, 0)),
        scratch_shapes=[pltpu.VMEM((D_MODEL, 2 * D_FF), BF16), pltpu.VMEM((D_FF, D_MODEL), BF16)],
    )
    return pl.pallas_call(
        _expert_kernel,
        grid_spec=grid_spec,
        out_shape=jax.ShapeDtypeStruct((nr, D_MODEL), BF16),
        compiler_params=pltpu.CompilerParams(dimension_semantics=("arbitrary",),
                                             vmem_limit_bytes=VMEM_LIMIT),
        name="experts",
    )(block_expert, n_used, rows, row_gate, wup, bup, wdn, bdn)


def _sort_trips(n_rows):
    step = SORT_ROWS * SORT_UNROLL
    return (n_rows + step - 1) // step


def _segment_copies(tile, len_ref, loc_ref, glob_ref, make_copies, act):
    def per_expert(e, carry):
        n = len_ref[tile * N_EXPERTS + e]
        loc = loc_ref[tile * N_EXPERTS + e]
        glob = glob_ref[tile * N_EXPERTS + e]
        off = jnp.int32(0)
        for bit in SEG_BITS:
            part = n & bit

            @pl.when(part != 0)
            def _():
                lo = pl.multiple_of(loc + off, SEG_ALIGN)
                go = pl.multiple_of(glob + off, SEG_ALIGN)
                for cp in make_copies(lo, go, bit):
                    act(cp)

            off = off + part
        return carry

    lax.fori_loop(0, N_EXPERTS, per_expert, 0)


def _dispatch_kernel(len_ref, loc_ref, glob_ref, tot_ref, nused_ref, grp_ref,
                     xb_ref, pose_ref, gatee_ref,
                     rows_hbm, gates_hbm, xs_ref, gs_ref, sel_ref, sem):
    i = pl.program_id(0)
    nt = pl.num_programs(0)
    slot = i % 2
    ts = xb_ref.shape[0]

    def copies(s):
        def make(lo, go, n):
            return (pltpu.make_async_copy(xs_ref.at[s, pl.ds(lo, n)], rows_hbm.at[pl.ds(go, n)], sem.at[0, s]),
                    pltpu.make_async_copy(gs_ref.at[s, pl.ds(lo, n)], gates_hbm.at[pl.ds(go, n)], sem.at[1, s]))
        return make

    tables = (len_ref, loc_ref, glob_ref)

    @pl.when(i >= 2)
    def _():
        _segment_copies(i - 2, *tables, copies(slot), lambda cp: cp.wait())

    xb = xb_ref[...]
    prow = lax.broadcasted_iota(jnp.int32, (SEG_ALIGN, ts), 0)

    def sort_rows(c, carry):
        for u in range(SORT_UNROLL):
            chunk = c * SORT_UNROLL + u
            r0 = pl.multiple_of(chunk * SORT_ROWS, SORT_ROWS)
            for g in range(SORT_ROWS // SEG_ALIGN):
                e = grp_ref[i * SORT_GROUPS + chunk * (SORT_ROWS // SEG_ALIGN) + g]
                p0 = r0 + g * SEG_ALIGN
                hit = (prow + p0) == pose_ref[pl.ds(e, 1), :]
                sel_ref[u, g * SEG_ALIGN:(g + 1) * SEG_ALIGN, :] = jnp.where(hit, 1.0, 0.0).astype(BF16)
                gsum = jnp.sum(jnp.where(hit, gatee_ref[pl.ds(e, 1), :], 0.0), axis=1, keepdims=True)
                gs_ref[slot, pl.ds(p0, SEG_ALIGN), :] = jnp.broadcast_to(gsum, (SEG_ALIGN, LANES))
            xs_ref[slot, pl.ds(r0, SORT_ROWS), :] = jnp.dot(
                sel_ref[u], xb, preferred_element_type=F32).astype(BF16)
        return carry

    lax.fori_loop(0, _sort_trips(tot_ref[i]), sort_rows, 0)

    @pl.when(i == nt - 1)
    def _():
        xs_ref[slot, 0:EXPERT_ROWS, :] = jnp.zeros((EXPERT_ROWS, D_MODEL), BF16)
        gs_ref[slot, 0:EXPERT_ROWS, :] = jnp.zeros((EXPERT_ROWS, LANES), F32)

    _segment_copies(i, *tables, copies(slot), lambda cp: cp.start())

    def unused_blocks(act):
        def per_block(j, carry):
            for cp in copies(slot)(0, pl.multiple_of(j * EXPERT_ROWS, EXPERT_ROWS), EXPERT_ROWS):
                act(cp)
            return carry
        lax.fori_loop(nused_ref[0], rows_hbm.shape[0] // EXPERT_ROWS, per_block, 0)

    @pl.when(i == nt - 1)
    def _():
        unused_blocks(lambda cp: cp.start())

        @pl.when(i >= 1)
        def _():
            _segment_copies(i - 1, *tables, copies(1 - slot), lambda cp: cp.wait())
        _segment_copies(i, *tables, copies(slot), lambda cp: cp.wait())
        unused_blocks(lambda cp: cp.wait())


def _dispatch(seg_len, seg_loc, seg_glob, tile_tot, n_used, grp_expert, x1b, pos_e, gate_e, n_rows):
    T = x1b.shape[0]
    ts = TOKEN_TILE
    nt = T // ts
    any_spec = pl.BlockSpec(memory_space=pl.ANY)

    def tile(i, *_):
        return jnp.minimum(i, nt - 1)

    grid_spec = pltpu.PrefetchScalarGridSpec(
        num_scalar_prefetch=6,
        grid=(nt + 1,),
        in_specs=[pl.BlockSpec((ts, D_MODEL), lambda i, *_: (tile(i), 0)),
                  pl.BlockSpec((N_EXPERTS, ts), lambda i, *_: (0, tile(i))),
                  pl.BlockSpec((N_EXPERTS, ts), lambda i, *_: (0, tile(i)))],
        out_specs=[any_spec, any_spec],
        scratch_shapes=[pltpu.VMEM((2, SORTED_ROWS, D_MODEL), BF16),
                        pltpu.VMEM((2, SORTED_ROWS, LANES), F32),
                        pltpu.VMEM((SORT_UNROLL, SORT_ROWS, ts), BF16),
                        pltpu.SemaphoreType.DMA((2, 2))],
    )
    return pl.pallas_call(
        _dispatch_kernel,
        grid_spec=grid_spec,
        out_shape=[jax.ShapeDtypeStruct((n_rows, D_MODEL), BF16), jax.ShapeDtypeStruct((n_rows, LANES), F32)],
        compiler_params=pltpu.CompilerParams(dimension_semantics=("arbitrary",),
                                             vmem_limit_bytes=VMEM_LIMIT),
        name="dispatch",
    )(seg_len, seg_loc, seg_glob, tile_tot, n_used, grp_expert, x1b, pos_e, gate_e)


def _combine_kernel(len_ref, loc_ref, glob_ref, tot_ref,
                    pos_ref, x1_ref, lng_ref, lnb_ref, y_hbm, o_ref, ybuf_ref, posb_ref, acc_ref, sem):
    i = pl.program_id(0)
    nt = pl.num_programs(0)
    slot = i % 2
    ts = x1_ref.shape[0]

    def copies(s):
        def make(lo, go, n):
            return (pltpu.make_async_copy(y_hbm.at[pl.ds(go, n)], ybuf_ref.at[s, pl.ds(lo, n)], sem.at[s]),)
        return make

    tables = (len_ref, loc_ref, glob_ref)

    @pl.when(i == 0)
    def _():
        ybuf_ref[...] = jnp.zeros(ybuf_ref.shape, ybuf_ref.dtype)
        _segment_copies(0, *tables, copies(0), lambda cp: cp.start())

    @pl.when(i + 1 < nt)
    def _():
        _segment_copies(i + 1, *tables, copies(1 - slot), lambda cp: cp.start())

    for k in range(TOP_K):
        posb_ref[k] = jnp.broadcast_to(pos_ref[:, k:k + 1], (ts, LANES))
    acc_ref[...] = jnp.zeros((ts, D_MODEL), F32)
    _segment_copies(i, *tables, copies(slot), lambda cp: cp.wait())
    lane = lax.broadcasted_iota(jnp.int32, (ts, LANES), 1)

    def gather_rows(c, carry):
        for u in range(SORT_UNROLL):
            r0 = pl.multiple_of((c * SORT_UNROLL + u) * SORT_ROWS, SORT_ROWS)
            halves = []
            for half in range(SORT_ROWS // LANES):
                col = lane + (r0 + half * LANES)
                hit = None
                for k in range(TOP_K):
                    hk = posb_ref[k] == col
                    hit = hk if hit is None else (hit | hk)
                halves.append(jnp.where(hit, 1.0, 0.0).astype(BF16))
            onehot = jnp.concatenate(halves, axis=1)
            acc_ref[...] += jnp.dot(onehot, ybuf_ref[slot, pl.ds(r0, SORT_ROWS), :],
                                    preferred_element_type=F32)
        return carry

    lax.fori_loop(0, _sort_trips(tot_ref[i]), gather_rows, 0)
    o_ref[...] = _layer_norm(DEEPNORM_ALPHA * x1_ref[...] + acc_ref[...], lng_ref[...], lnb_ref[...])


def _combine(seg_len, seg_loc, seg_glob, tile_tot, pos, x1, lng, lnb, y):
    T = x1.shape[0]
    ts = TOKEN_TILE
    grid_spec = pltpu.PrefetchScalarGridSpec(
        num_scalar_prefetch=4,
        grid=(T // ts,),
        in_specs=[pl.BlockSpec((ts, TOP_K), lambda i, *_: (i, 0)),
                  pl.BlockSpec((ts, D_MODEL), lambda i, *_: (i, 0)),
                  pl.BlockSpec((1, D_MODEL), lambda i, *_: (0, 0)),
                  pl.BlockSpec((1, D_MODEL), lambda i, *_: (0, 0)),
                  pl.BlockSpec(memory_space=pl.ANY)],
        out_specs=pl.BlockSpec((ts, D_MODEL), lambda i, *_: (i, 0)),
        scratch_shapes=[pltpu.VMEM((2, SORTED_ROWS, D_MODEL), BF16),
                        pltpu.VMEM((TOP_K, ts, LANES), jnp.int32),
                        pltpu.VMEM((ts, D_MODEL), F32),
                        pltpu.SemaphoreType.DMA((2,))],
    )
    return pl.pallas_call(
        _combine_kernel,
        grid_spec=grid_spec,
        out_shape=jax.ShapeDtypeStruct((T, D_MODEL), F32),
        compiler_params=pltpu.CompilerParams(dimension_semantics=("arbitrary",),
                                             vmem_limit_bytes=VMEM_LIMIT),
        name="combine",
    )(seg_len, seg_loc, seg_glob, tile_tot, pos, x1, lng, lnb, y)


def _moe(layer, x1, x1b, pos_t, pos_e, gate_e, cnt, wup, bup, wdn, bdn, lng, lnb):
    T = x1.shape[0]
    ts = TOKEN_TILE
    nt = T // ts
    r = EXPERT_ROWS
    n_blocks = -(-(T * TOP_K + nt * N_EXPERTS * (SEG_ALIGN - 1) + N_EXPERTS * (r - 1)) // r)
    counts = cnt[:, 0].reshape(nt, N_EXPERTS).astype(jnp.int32)
    seg_len = (counts + SEG_ALIGN - 1) // SEG_ALIGN * SEG_ALIGN
    seg_loc = jnp.cumsum(seg_len, axis=1) - seg_len
    tile_tot = jnp.sum(seg_len, axis=1)
    region = jnp.sum(seg_len, axis=0)
    region = (region + r - 1) // r * r
    region_end = jnp.cumsum(region)
    seg_glob = (region_end - region)[None, :] + jnp.cumsum(seg_len, axis=0) - seg_len
    n_used = (region_end[-1:] // r).astype(jnp.int32)
    block_start = jnp.arange(n_blocks, dtype=jnp.int32) * r
    block_expert = jnp.minimum(jnp.sum(region_end[None, :] <= block_start[:, None], axis=1),
                               N_EXPERTS - 1).astype(jnp.int32)
    used = jnp.sum(seg_len, axis=0)
    zero_row = jnp.zeros((1, N_EXPERTS), jnp.int32)
    tables = (jnp.concatenate([seg_len, (region - used)[None, :]]).reshape(-1),
              jnp.concatenate([seg_loc, zero_row]).reshape(-1),
              jnp.concatenate([seg_glob, (region_end - region + used)[None, :]]).reshape(-1),
              jnp.concatenate([tile_tot, jnp.zeros((1,), jnp.int32)]))
    group_start = jnp.arange(SORT_GROUPS, dtype=jnp.int32) * SEG_ALIGN
    seg_end = seg_loc + seg_len
    grp_expert = jnp.minimum(jnp.sum(seg_end[:, None, :] <= group_start[None, :, None], axis=2), N_EXPERTS - 1)
    grp_expert = jnp.concatenate([grp_expert, jnp.zeros((1, SORT_GROUPS), jnp.int32)]).astype(jnp.int32)
    rows, row_gate = _dispatch(*tables, n_used, grp_expert.reshape(-1), x1b, pos_e, gate_e, n_blocks * r)
    y = _experts(layer, block_expert, n_used, rows, row_gate, wup, bup, wdn, bdn)
    return _combine(*tables, pos_t.T, x1, lng, lnb, y)


def kernel(x, mem, positions, a_w_in, a_b_in, a_dw, a_dw_b, a_cn_g, a_cn_b, a_w_out, a_b_out, w_kv_shared, b_w_q, b_w_out, b_b_out, mem_w_kv, ln_g, ln_b, router_w, router_b, exp_w_up, exp_b_up, exp_w_down, exp_b_down):
    B, S, D = x.shape
    T = B * S
    assert D == D_MODEL and S % DIL_TILE == 0 and S % TOKEN_TILE == 0
    xt = x.reshape(T, D)
    kst, vst = _mem_kv(mem.astype(BF16), mem_w_kv.astype(BF16))
    pos = positions.reshape(T, 1).astype(jnp.int32)
    half = ROT_DIM // 2
    inv_freq = jnp.power(ROPE_THETA, -jnp.arange(half, dtype=F32) / half)
    e = jnp.arange(LANES) % HEAD_DIM
    freq = jnp.where(e < ROT_DIM, inv_freq[e % half], 0.0).reshape(1, LANES).astype(F32)
    dw_pad = jnp.pad(a_dw, ((0, 0), (0, 32 - CONV_WIDTH), (0, 0)))
    b_up = exp_b_up.reshape(DEPTH, N_EXPERTS, 1, 2 * D_FF)
    b_down = exp_b_down.reshape(DEPTH, N_EXPERTS, 1, D_MODEL)
    shared_kv = None
    for l in range(DEPTH):
        wrt = router_w[l].T
        br = router_b[l].reshape(N_EXPERTS, 1)
        lng0, lnb0 = _row(ln_g[l, 0]), _row(ln_b[l, 0])
        if l < N_A_LAYERS:
            outs = _layer_a(xt, S, a_w_in[l].astype(BF16), _row(a_b_in[l]), dw_pad[l], _row(a_dw_b[l]),
                            _row(a_cn_g[l]), _row(a_cn_b[l]), kst[l], vst[l],
                            a_w_out[l].astype(BF16), _row(a_b_out[l]), lng0, lnb0, wrt, br)
        else:
            j = l - N_A_LAYERS
            if j == 0:
                q, qm, k, v = _layer_b_proj(xt, B, S, pos, freq, b_w_q[j].astype(BF16), w_kv_shared.astype(BF16))
                shared_kv = (k, v)
            else:
                q, qm = _layer_b_proj(xt, B, S, pos, freq, b_w_q[j].astype(BF16), None)
            dil = []
            for g, (window, dilation) in enumerate(DIL_GROUPS):
                assert window // dilation == DIL_BLOCK and DIL_TILE % (DIL_BLOCK * dilation) == 0
                dil += _dilated_group(g, dilation, q, shared_kv[0], shared_kv[1])
            outs = _layer_b_out(xt, S, dil, qm, kst[l], vst[l], b_w_out[j].astype(BF16), _row(b_b_out[j]),
                                lng0, lnb0, wrt, br)
        x1, x1b, pos_t, pos_e, gate_e, cnt = outs
        xt = _moe(l, x1, x1b, pos_t, pos_e, gate_e, cnt, exp_w_up, b_up, exp_w_down, b_down,
                  _row(ln_g[l, 1]), _row(ln_b[l, 1]))
    return xt.reshape(B, S, D)
```

```python
import functools

import jax
import jax.numpy as jnp
from jax import lax
from jax.experimental import pallas as pl
from jax.experimental.pallas import tpu as pltpu

D_MODEL = 1024
DEPTH = 4
N_A_LAYERS = 2
HEAD_DIM = 64
ROT_DIM = 16
ROPE_THETA = 500000.0
CONV_CH = 768
CONV_WIDTH = 31
MEM_TOKENS = 256
MEM_WIDTH = 256
DIL_GROUPS = ((128, 1), (512, 4), (2048, 16))
DIL_WIDTH = 256
DIL_BLOCK = 128
N_EXPERTS = 32
TOP_K = 4
D_FF = 1024
SWIGLU_ALPHA = 1.702
SWIGLU_LIMIT = 7.0
DEEPNORM_ALPHA = (2 * DEPTH) ** 0.25
LN_EPS = 1e-5
ATTN_SCALE = HEAD_DIM ** -0.5

LANES = 128
TOKEN_TILE = 512
DIL_TILE = 2048
EXPERT_ROWS = 512
CONV_ROWS = 64
CONV_LANES = 384
SEG_ALIGN = 16
SEG_BITS = tuple(SEG_ALIGN << b for b in reversed(range(6)))
SORT_ROWS = 256
SORT_UNROLL = 5
SORTED_ROWS = 2560
SORT_GROUPS = SORTED_ROWS // SEG_ALIGN
VMEM_LIMIT = 56 * 1024 * 1024

F32 = jnp.float32
BF16 = jnp.bfloat16
NEG = -1e30


def _layer_norm(v, g, b):
    mu = jnp.mean(v, axis=-1, keepdims=True)
    c = v - mu
    var = jnp.mean(c * c, axis=-1, keepdims=True)
    return c * lax.rsqrt(var + LN_EPS) * g + b


def _nt_dot(a, b, **kw):
    return lax.dot_general(a, b, (((1,), (1,)), ((), ())), preferred_element_type=F32, **kw)


def _mem_attention(qm_bf16, kst_ref, vst_ref):
    out = None
    for h in range(4):
        s = _nt_dot(qm_bf16, kst_ref[h])
        m = jnp.max(s, axis=-1, keepdims=True)
        p = jnp.exp(s - m)
        l = jnp.sum(p, axis=-1, keepdims=True)
        pn = (p * (1.0 / l)).astype(BF16)
        u = jnp.dot(pn, vst_ref[h], preferred_element_type=F32)
        out = u if out is None else out + u
    return out


def _residual_ln_router(x_res, y, lng_ref, lnb_ref, wrt_ref, br_ref,
                        x1_ref, x1b_ref, pos_ref, pose_ref, gatee_ref, cnt_ref):
    ts = x_res.shape[0]
    x1 = _layer_norm(DEEPNORM_ALPHA * x_res + y, lng_ref[...], lnb_ref[...])
    x1_ref[...] = x1
    x1b_ref[...] = x1.astype(BF16)
    logit = _nt_dot(wrt_ref[...], x1, precision=lax.Precision.HIGHEST) + br_ref[...]
    row = lax.broadcasted_iota(jnp.int32, (N_EXPERTS, ts), 0)
    vals, onehots = [], []
    for k in range(TOP_K):
        m = jnp.max(logit, axis=0, keepdims=True)
        idx = jnp.min(jnp.where(logit == m, row, N_EXPERTS), axis=0, keepdims=True)
        oh = row == idx
        logit = jnp.where(oh, -jnp.inf, logit)
        vals.append(m)
        onehots.append(oh)
    exps = [jnp.exp(v - vals[0]) for v in vals]
    inv = 1.0 / (exps[0] + exps[1] + exps[2] + exps[3])
    gate_e = jnp.zeros((N_EXPERTS, ts), F32)
    for k in range(TOP_K):
        gate_e = jnp.where(onehots[k], exps[k] * inv, gate_e)
    gatee_ref[...] = gate_e
    oh_all = (onehots[0] | onehots[1] | onehots[2] | onehots[3])
    oh_f = jnp.where(oh_all, 1.0, 0.0)
    ri = lax.broadcasted_iota(jnp.int32, (ts, ts), 0)
    ci = lax.broadcasted_iota(jnp.int32, (ts, ts), 1)
    tri = jnp.where(ri < ci, 1.0, 0.0).astype(BF16)
    cum = jnp.dot(oh_f.astype(BF16), tri, preferred_element_type=F32)
    cnt = jnp.sum(oh_f, axis=1, keepdims=True)
    cnt_b = jnp.broadcast_to(cnt, (N_EXPERTS, LANES))
    cnt_ref[...] = cnt_b
    units = jnp.floor((cnt_b + (SEG_ALIGN - 1)) * (1.0 / SEG_ALIGN))
    er = lax.broadcasted_iota(jnp.int32, (N_EXPERTS, N_EXPERTS), 0)
    ec = lax.broadcasted_iota(jnp.int32, (N_EXPERTS, N_EXPERTS), 1)
    low_tri = jnp.where(ec < er, 1.0, 0.0).astype(BF16)
    seg_start = jnp.dot(low_tri, units.astype(BF16), preferred_element_type=F32)[:, 0:1] * SEG_ALIGN
    pos_all = cum + seg_start
    pose_ref[...] = jnp.where(oh_all, pos_all, -1.0).astype(jnp.int32)
    for k in range(TOP_K):
        rk = jnp.sum(jnp.where(onehots[k], pos_all, 0.0), axis=0, keepdims=True)
        pos_ref[k:k + 1, :] = rk.astype(jnp.int32)


def _mem_kv_kernel(mem_ref, w_ref, kst_ref, vst_ref):
    kv = jnp.dot(mem_ref[...], w_ref[...], preferred_element_type=F32)
    k = kv[:, :MEM_WIDTH] * ATTN_SCALE
    v = kv[:, MEM_WIDTH:]
    head = lax.broadcasted_iota(jnp.int32, (MEM_TOKENS, MEM_WIDTH), 1) // HEAD_DIM
    for h in range(4):
        kst_ref[h] = jnp.where(head == h, k, 0.0).astype(BF16)
        vst_ref[h] = jnp.where(head == h, v, 0.0).astype(BF16)


def _mem_kv(mem_b, w_b):
    B = mem_b.shape[0]
    out = jax.ShapeDtypeStruct((DEPTH, B, 4, MEM_TOKENS, MEM_WIDTH), BF16)
    return pl.pallas_call(
        _mem_kv_kernel,
        grid=(DEPTH, B),
        in_specs=[pl.BlockSpec((None, MEM_TOKENS, D_MODEL), lambda l, b: (b, 0, 0)),
                  pl.BlockSpec((None, D_MODEL, 2 * MEM_WIDTH), lambda l, b: (l, 0, 0))],
        out_specs=[pl.BlockSpec((None, None, 4, MEM_TOKENS, MEM_WIDTH), lambda l, b: (l, b, 0, 0, 0))] * 2,
        out_shape=[out, out],
        name="mem_kv",
    )(mem_b, w_b)


def _layer_a_kernel(tiles_per_batch,
                    x_ref, win_ref, bin_ref, dw_ref, dwb_ref, cng_ref, cnb_ref, kst_ref, vst_ref,
                    wout_ref, bout_ref, lng_ref, lnb_ref, wrt_ref, br_ref,
                    x1_ref, x1b_ref, pos_ref, pose_ref, gatee_ref, cnt_ref,
                    hpad_ref, shift_ref, cat_ref):
    ts = x_ref.shape[0]
    i = pl.program_id(0)
    x = x_ref[...]
    h = jnp.dot(x.astype(BF16), win_ref[...], preferred_element_type=F32) + bin_ref[...]
    hg = h[:, :CONV_CH] * jax.nn.sigmoid(h[:, CONV_CH:2 * CONV_CH])
    qm = h[:, 2 * CONV_CH:].astype(BF16)

    @pl.when(i % tiles_per_batch == 0)
    def _():
        hpad_ref[0:32, :] = jnp.zeros((32, CONV_CH), F32)

    hpad_ref[32:32 + ts, :] = hg
    for b in range(8):
        n = ts + 8 * ((CONV_WIDTH - 1 - b) // 8)
        for r0 in range(0, n, 128):
            rows = min(128, n - r0)
            shift_ref[b, r0:r0 + rows, :] = hpad_ref[pl.ds(2 + b + r0, rows), :]
    hpad_ref[0:32, :] = hpad_ref[ts:ts + 32, :]

    for c0 in range(0, CONV_CH, CONV_LANES):
        lanes = slice(c0, c0 + CONV_LANES)

        def conv_rows(c, carry, lanes=lanes):
            r0 = pl.multiple_of(c * CONV_ROWS, CONV_ROWS)
            acc = jnp.zeros((CONV_ROWS, CONV_LANES), F32) + dwb_ref[:, lanes]
            for k in range(CONV_WIDTH):
                a, b = divmod(k, 8)
                acc = acc + shift_ref[b, pl.ds(r0 + 8 * a, CONV_ROWS), lanes] * dw_ref[k:k + 1, lanes]
            hpad_ref[pl.ds(32 + r0, CONV_ROWS), lanes] = acc
            return carry

        lax.fori_loop(0, ts // CONV_ROWS, conv_rows, 0)

    def norm_rows(c, carry):
        r0 = pl.multiple_of(c * CONV_ROWS, CONV_ROWS)
        cn = _layer_norm(hpad_ref[pl.ds(32 + r0, CONV_ROWS), :], cng_ref[...], cnb_ref[...])
        cat_ref[pl.ds(r0, CONV_ROWS), 0:CONV_CH] = (cn * jax.nn.sigmoid(cn)).astype(BF16)
        return carry

    lax.fori_loop(0, ts // CONV_ROWS, norm_rows, 0)

    cat_ref[:, CONV_CH:] = _mem_attention(qm, kst_ref, vst_ref).astype(BF16)
    y = jnp.dot(cat_ref[...], wout_ref[...], preferred_element_type=F32) + bout_ref[...]
    _residual_ln_router(x, y, lng_ref, lnb_ref, wrt_ref, br_ref,
                        x1_ref, x1b_ref, pos_ref, pose_ref, gatee_ref, cnt_ref)


def _row(v):
    return v.reshape(1, -1).astype(F32)


def _const_spec(shape):
    nd = len(shape)
    return pl.BlockSpec(shape, lambda i: (0,) * nd)


def _router_out(T, ts):
    nt = T // ts
    shapes = [jax.ShapeDtypeStruct((T, D_MODEL), F32), jax.ShapeDtypeStruct((T, D_MODEL), BF16),
              jax.ShapeDtypeStruct((TOP_K, T), jnp.int32), jax.ShapeDtypeStruct((N_EXPERTS, T), jnp.int32),
              jax.ShapeDtypeStruct((N_EXPERTS, T), F32), jax.ShapeDtypeStruct((nt * N_EXPERTS, LANES), F32)]
    specs = [pl.BlockSpec((ts, D_MODEL), lambda i: (i, 0)), pl.BlockSpec((ts, D_MODEL), lambda i: (i, 0)),
             pl.BlockSpec((TOP_K, ts), lambda i: (0, i)), pl.BlockSpec((N_EXPERTS, ts), lambda i: (0, i)),
             pl.BlockSpec((N_EXPERTS, ts), lambda i: (0, i)), pl.BlockSpec((N_EXPERTS, LANES), lambda i: (i, 0))]
    return shapes, specs


def _layer_a(x, S, win, bin_, dw, dwb, cng, cnb, kst, vst, wout, bout, lng, lnb, wrt, br):
    T = x.shape[0]
    ts = TOKEN_TILE
    tpb = S // ts
    out_shapes, out_specs = _router_out(T, ts)
    kv_spec = pl.BlockSpec((None, 4, MEM_TOKENS, MEM_WIDTH), lambda i: (i // tpb, 0, 0, 0))
    a_in = 2 * CONV_CH + MEM_WIDTH
    return pl.pallas_call(
        functools.partial(_layer_a_kernel, tpb),
        grid=(T // ts,),
        in_specs=[pl.BlockSpec((ts, D_MODEL), lambda i: (i, 0)),
                  _const_spec((D_MODEL, a_in)), _const_spec((1, a_in)),
                  _const_spec((32, CONV_CH)), _const_spec((1, CONV_CH)),
                  _const_spec((1, CONV_CH)), _const_spec((1, CONV_CH)),
                  kv_spec, kv_spec,
                  _const_spec((D_MODEL, D_MODEL)), _const_spec((1, D_MODEL)),
                  _const_spec((1, D_MODEL)), _const_spec((1, D_MODEL)),
                  _const_spec((N_EXPERTS, D_MODEL)), _const_spec((N_EXPERTS, 1))],
        out_specs=out_specs,
        out_shape=out_shapes,
        scratch_shapes=[pltpu.VMEM((ts + 32, CONV_CH), F32),
                        pltpu.VMEM((8, ts + 24, CONV_CH), F32),
                        pltpu.VMEM((ts, D_MODEL), BF16)],
        compiler_params=pltpu.CompilerParams(dimension_semantics=("arbitrary",),
                                             vmem_limit_bytes=VMEM_LIMIT),
        name="layer_a",
    )(x, win, bin_, dw, dwb, cng, cnb, kst, vst, wout, bout, lng, lnb, wrt, br)


def _rotary(v, cos_t, sin_t, low):
    partner = jnp.where(low, pltpu.roll(v, LANES - 8, axis=1), pltpu.roll(v, 8, axis=1))
    return v * cos_t + partner * sin_t


def _rope_tables(pos_ref, freq_ref):
    e = lax.broadcasted_iota(jnp.int32, (1, LANES), 1) % HEAD_DIM
    ang = pos_ref[...].astype(F32) * freq_ref[...]
    low = e < ROT_DIM // 2
    rot = e < ROT_DIM
    cos_t = jnp.where(rot, jnp.cos(ang), 1.0)
    sn = jnp.sin(ang)
    sin_t = jnp.where(low, -sn, jnp.where(rot, sn, 0.0))
    return cos_t, sin_t, low


def _layer_b_proj_kernel(with_kv, x_ref, pos_ref, freq_ref, wq_ref, *rest):
    if with_kv:
        wkv_ref, q_ref, qm_ref, k_ref, v_ref = rest
    else:
        q_ref, qm_ref = rest
    xb = x_ref[...].astype(BF16)
    cos_t, sin_t, low = _rope_tables(pos_ref, freq_ref)
    q = jnp.dot(xb, wq_ref[...], preferred_element_type=F32)
    for g in range(3):
        for s in range(2):
            c0 = g * DIL_WIDTH + s * LANES
            q_ref[g, s] = _rotary(q[:, c0:c0 + LANES], cos_t, sin_t, low) * ATTN_SCALE
    qm_ref[...] = q[:, 3 * DIL_WIDTH:].astype(BF16)
    if with_kv:
        kv = jnp.dot(xb, wkv_ref[...], preferred_element_type=F32)
        for g in range(3):
            for s in range(2):
                c0 = g * 2 * DIL_WIDTH + s * LANES
                k_ref[g, s] = _rotary(kv[:, c0:c0 + LANES], cos_t, sin_t, low)
                v_ref[g, s] = kv[:, c0 + DIL_WIDTH:c0 + DIL_WIDTH + LANES]


def _layer_b_proj(x, B, S, pos, freq, wq, wkv):
    T = x.shape[0]
    ts = TOKEN_TILE
    tpb = S // ts
    with_kv = wkv is not None
    slab = jax.ShapeDtypeStruct((3, B, 2, S, LANES), F32)
    slab_spec = pl.BlockSpec((3, None, 2, ts, LANES), lambda i: (0, i // tpb, 0, i % tpb, 0))
    in_specs = [pl.BlockSpec((ts, D_MODEL), lambda i: (i, 0)),
                pl.BlockSpec((ts, 1), lambda i: (i, 0)),
                _const_spec((1, LANES)),
                _const_spec((D_MODEL, D_MODEL))]
    args = [x, pos, freq, wq]
    out_shape = [slab, jax.ShapeDtypeStruct((T, MEM_WIDTH), BF16)]
    out_specs = [slab_spec, pl.BlockSpec((ts, MEM_WIDTH), lambda i: (i, 0))]
    if with_kv:
        in_specs.append(_const_spec((D_MODEL, 6 * DIL_WIDTH)))
        args.append(wkv)
        out_shape += [slab, slab]
        out_specs += [slab_spec, slab_spec]
    return pl.pallas_call(
        functools.partial(_layer_b_proj_kernel, with_kv),
        grid=(T // ts,),
        in_specs=in_specs, out_specs=out_specs, out_shape=out_shape,
        compiler_params=pltpu.CompilerParams(dimension_semantics=("arbitrary",),
                                             vmem_limit_bytes=VMEM_LIMIT),
        name="layer_b_proj_kv" if with_kv else "layer_b_proj",
    )(*args)


def _dilated_kernel(d, q_ref, k_ref, v_ref, o_ref, lse_ref, kbuf_ref, vbuf_ref):
    i = pl.program_id(1)
    tile = DIL_TILE
    blk = DIL_BLOCK

    @pl.when(i == 0)
    def _():
        kbuf_ref[:, 0:tile, :] = jnp.zeros((2, tile, LANES), F32)
        vbuf_ref[:, 0:tile, :] = jnp.zeros((2, tile, LANES), F32)

    kbuf_ref[:, tile:, :] = k_ref[...]
    vbuf_ref[:, tile:, :] = v_ref[...]

    def rows(start, n):
        return pl.ds(start, n) if d == 1 else pl.ds(start, n, stride=d)

    qi = lax.broadcasted_iota(jnp.int32, (blk, 2 * blk), 0)
    ki = lax.broadcasted_iota(jnp.int32, (blk, 2 * blk), 1)
    dist = qi + blk - ki
    band = (dist >= 0) & (dist <= blk)
    head = lax.broadcasted_iota(jnp.int32, (1, LANES), 1) // HEAD_DIM

    def block(j, carry):
        c = j // d
        r = j % d
        qs = c * (blk * d) + r
        ks = tile + qs - blk * d
        first = jnp.logical_and(i == 0, c == 0)
        valid = band & jnp.logical_or(jnp.logical_not(first), ki >= blk)
        for s in range(2):
            q = q_ref[s, rows(qs, blk), :].astype(BF16)
            kk = kbuf_ref[s, rows(ks, 2 * blk), :]
            vv = vbuf_ref[s, rows(ks, 2 * blk), :]
            o_acc = jnp.zeros((blk, LANES), F32)
            lse_acc = jnp.zeros((blk, LANES), F32)
            for h in range(2):
                hm = head == h
                sc = _nt_dot(q, jnp.where(hm, kk, 0.0).astype(BF16))
                sc = jnp.where(valid, sc, NEG)
                m = jnp.max(sc, axis=-1, keepdims=True)
                p = jnp.exp(sc - m)
                l = jnp.sum(p, axis=-1, keepdims=True)
                u = jnp.dot(p.astype(BF16), jnp.where(hm, vv, 0.0).astype(BF16),
                            preferred_element_type=F32)
                o_acc = o_acc + u * (1.0 / l)
                lse_acc = jnp.where(hm, m + jnp.log(l), lse_acc)
            o_ref[s, rows(qs, blk), :] = o_acc
            lse_ref[s, rows(qs, blk), :] = lse_acc
        return carry

    lax.fori_loop(0, tile // blk, block, 0, unroll=4)
    kbuf_ref[:, 0:tile, :] = kbuf_ref[:, tile:, :]
    vbuf_ref[:, 0:tile, :] = vbuf_ref[:, tile:, :]


def _dilated_group(g, d, q, k, v):
    _, B, _, S, _ = q.shape
    tile = DIL_TILE
    in_spec = pl.BlockSpec((None, None, 2, tile, LANES), lambda b, i: (g, b, 0, i, 0))
    out_spec = pl.BlockSpec((None, 2, tile, LANES), lambda b, i: (b, 0, i, 0))
    out = jax.ShapeDtypeStruct((B, 2, S, LANES), F32)
    return pl.pallas_call(
        functools.partial(_dilated_kernel, d),
        grid=(B, S // tile),
        in_specs=[in_spec, in_spec, in_spec],
        out_specs=[out_spec, out_spec],
        out_shape=[out, out],
        scratch_shapes=[pltpu.VMEM((2, 2 * tile, LANES), F32), pltpu.VMEM((2, 2 * tile, LANES), F32)],
        compiler_params=pltpu.CompilerParams(dimension_semantics=("arbitrary", "arbitrary"),
                                             vmem_limit_bytes=VMEM_LIMIT),
        name=f"dilated_d{d}",
    )(q, k, v)


def _layer_b_out_kernel(x_ref, o0_ref, l0_ref, o1_ref, l1_ref, o2_ref, l2_ref, qm_ref, kst_ref, vst_ref,
                        wout_ref, bout_ref, lng_ref, lnb_ref, wrt_ref, br_ref,
                        x1_ref, x1b_ref, pos_ref, pose_ref, gatee_ref, cnt_ref, cat_ref):
    for s in range(2):
        lses = [l0_ref[s], l1_ref[s], l2_ref[s]]
        outs = [o0_ref[s], o1_ref[s], o2_ref[s]]
        m = jnp.maximum(jnp.maximum(lses[0], lses[1]), lses[2])
        es = [jnp.exp(l - m) for l in lses]
        inv = 1.0 / (es[0] + es[1] + es[2])
        dil = (es[0] * inv) * outs[0] + (es[1] * inv) * outs[1] + (es[2] * inv) * outs[2]
        cat_ref[:, s * LANES:(s + 1) * LANES] = dil.astype(BF16)
    cat_ref[:, DIL_WIDTH:] = _mem_attention(qm_ref[...], kst_ref, vst_ref).astype(BF16)
    y = jnp.dot(cat_ref[...], wout_ref[...], preferred_element_type=F32) + bout_ref[...]
    _residual_ln_router(x_ref[...], y, lng_ref, lnb_ref, wrt_ref, br_ref,
                        x1_ref, x1b_ref, pos_ref, pose_ref, gatee_ref, cnt_ref)


def _layer_b_out(x, S, dil, qm, kst, vst, wout, bout, lng, lnb, wrt, br):
    T = x.shape[0]
    ts = TOKEN_TILE
    tpb = S // ts
    out_shapes, out_specs = _router_out(T, ts)
    kv_spec = pl.BlockSpec((None, 4, MEM_TOKENS, MEM_WIDTH), lambda i: (i // tpb, 0, 0, 0))
    slab_spec = pl.BlockSpec((None, 2, ts, LANES), lambda i: (i // tpb, 0, i % tpb, 0))
    b_out = DIL_WIDTH + MEM_WIDTH
    return pl.pallas_call(
        _layer_b_out_kernel,
        grid=(T // ts,),
        in_specs=[pl.BlockSpec((ts, D_MODEL), lambda i: (i, 0))] + [slab_spec] * 6 + [
            pl.BlockSpec((ts, MEM_WIDTH), lambda i: (i, 0)), kv_spec, kv_spec,
            _const_spec((b_out, D_MODEL)), _const_spec((1, D_MODEL)),
            _const_spec((1, D_MODEL)), _const_spec((1, D_MODEL)),
            _const_spec((N_EXPERTS, D_MODEL)), _const_spec((N_EXPERTS, 1))],
        out_specs=out_specs,
        out_shape=out_shapes,
        scratch_shapes=[pltpu.VMEM((ts, b_out), BF16)],
        compiler_params=pltpu.CompilerParams(dimension_semantics=("arbitrary",),
                                             vmem_limit_bytes=VMEM_LIMIT),
        name="layer_b_out",
    )(x, *dil, qm, kst, vst, wout, bout, lng, lnb, wrt, br)


def _used_block(i, nu):
    return jnp.maximum(jnp.minimum(i, nu[0] - 1), 0)


def _expert_kernel(be_ref, nused_ref, valid_ref, rows_ref, gate_ref, wup_ref, bup_ref, wdn_ref, bdn_ref, y_ref,
                   wup_bf_ref, wdn_bf_ref):
    i = pl.program_id(0)
    used = i < nused_ref[0]
    half = rows_ref.shape[0] // 2

    @pl.when(jnp.logical_not(used))
    def _():
        y_ref[...] = jnp.zeros(y_ref.shape, y_ref.dtype)

    b = _used_block(i, nused_ref)
    new_expert = jnp.logical_or(i == 0, be_ref[b] != be_ref[jnp.maximum(b - 1, 0)])

    @pl.when(jnp.logical_and(used, new_expert))
    def _():
        wup_bf_ref[...] = wup_ref[...].astype(BF16)
        wdn_bf_ref[...] = wdn_ref[...].astype(BF16)

    def ffn(rs):
        h = jnp.dot(rows_ref[rs, :], wup_bf_ref[...], preferred_element_type=F32) + bup_ref[...]
        g = jnp.minimum(h[:, :D_FF], SWIGLU_LIMIT)
        lin = jnp.clip(h[:, D_FF:], -SWIGLU_LIMIT, SWIGLU_LIMIT)
        act = g * jax.nn.sigmoid(SWIGLU_ALPHA * g) * (lin + 1.0)
        y = jnp.dot(act.astype(BF16), wdn_bf_ref[...], preferred_element_type=F32) + bdn_ref[...]
        gate = gate_ref[rs, :]
        for c in range(D_MODEL // LANES):
            y_ref[rs, c * LANES:(c + 1) * LANES] = (
                y[:, c * LANES:(c + 1) * LANES] * gate).astype(y_ref.dtype)

    data_rows = valid_ref[b]

    @pl.when(jnp.logical_and(used, data_rows > half))
    def _():
        ffn(slice(None))

    @pl.when(jnp.logical_and(used, data_rows <= half))
    def _():
        ffn(slice(0, half))
        y_ref[half:, :] = jnp.zeros((y_ref.shape[0] - half, D_MODEL), y_ref.dtype)


def _experts(layer, block_expert, n_used, block_rows, rows, row_gate, wup, bup, wdn, bdn):
    nr = rows.shape[0]
    r = EXPERT_ROWS

    def row_map(i, be, nu, br):
        return (_used_block(i, nu), 0)

    def wmap(i, be, nu, br):
        return (layer, be[_used_block(i, nu)], 0, 0)

    grid_spec = pltpu.PrefetchScalarGridSpec(
        num_scalar_prefetch=3,
        grid=(nr // r,),
        in_specs=[pl.BlockSpec((r, D_MODEL), row_map),
                  pl.BlockSpec((r, LANES), row_map),
                  pl.BlockSpec((None, None, D_MODEL, 2 * D_FF), wmap),
                  pl.BlockSpec((None, None, 1, 2 * D_FF), wmap),
                  pl.BlockSpec((None, None, D_FF, D_MODEL), wmap),
                  pl.BlockSpec((None, None, 1, D_MODEL), wmap)],
        out_specs=pl.BlockSpec((r, D_MODEL), lambda i, be, nu, br: (i, 0)),
        scratch_shapes=[pltpu.VMEM((D_MODEL, 2 * D_FF), BF16), pltpu.VMEM((D_FF, D_MODEL), BF16)],
    )
    return pl.pallas_call(
        _expert_kernel,
        grid_spec=grid_spec,
        out_shape=jax.ShapeDtypeStruct((nr, D_MODEL), BF16),
        compiler_params=pltpu.CompilerParams(dimension_semantics=("arbitrary",),
                                             vmem_limit_bytes=VMEM_LIMIT),
        name="experts",
    )(block_expert, n_used, block_rows, rows, row_gate, wup, bup, wdn, bdn)


def _sort_trips(n_rows):
    step = SORT_ROWS * SORT_UNROLL
    return (n_rows + step - 1) // step


def _segment_copies(tile, len_ref, loc_ref, glob_ref, make_copies, act):
    def per_expert(e, carry):
        n = len_ref[tile * N_EXPERTS + e]
        loc = loc_ref[tile * N_EXPERTS + e]
        glob = glob_ref[tile * N_EXPERTS + e]
        off = jnp.int32(0)
        for bit in SEG_BITS:
            part = n & bit

            @pl.when(part != 0)
            def _():
                lo = pl.multiple_of(loc + off, SEG_ALIGN)
                go = pl.multiple_of(glob + off, SEG_ALIGN)
                for cp in make_copies(lo, go, bit):
                    act(cp)

            off = off + part
        return carry

    lax.fori_loop(0, N_EXPERTS, per_expert, 0)


def _dispatch_kernel(len_ref, loc_ref, glob_ref, tot_ref, nused_ref, grp_ref,
                     xb_ref, pose_ref, gatee_ref,
                     rows_hbm, gates_hbm, xs_ref, gs_ref, sel_ref, sem):
    i = pl.program_id(0)
    nt = pl.num_programs(0)
    slot = i % 2
    ts = xb_ref.shape[0]

    def copies(s):
        def make(lo, go, n):
            return (pltpu.make_async_copy(xs_ref.at[s, pl.ds(lo, n)], rows_hbm.at[pl.ds(go, n)], sem.at[0, s]),
                    pltpu.make_async_copy(gs_ref.at[s, pl.ds(lo, n)], gates_hbm.at[pl.ds(go, n)], sem.at[1, s]))
        return make

    tables = (len_ref, loc_ref, glob_ref)

    @pl.when(i >= 2)
    def _():
        _segment_copies(i - 2, *tables, copies(slot), lambda cp: cp.wait())

    xb = xb_ref[...]
    prow = lax.broadcasted_iota(jnp.int32, (SEG_ALIGN, ts), 0)

    def sort_rows(c, carry):
        for u in range(SORT_UNROLL):
            chunk = c * SORT_UNROLL + u
            r0 = pl.multiple_of(chunk * SORT_ROWS, SORT_ROWS)
            for g in range(SORT_ROWS // SEG_ALIGN):
                e = grp_ref[i * SORT_GROUPS + chunk * (SORT_ROWS // SEG_ALIGN) + g]
                p0 = r0 + g * SEG_ALIGN
                hit = (prow + p0) == pose_ref[pl.ds(e, 1), :]
                sel_ref[u, g * SEG_ALIGN:(g + 1) * SEG_ALIGN, :] = jnp.where(hit, 1.0, 0.0).astype(BF16)
                gsum = jnp.sum(jnp.where(hit, gatee_ref[pl.ds(e, 1), :], 0.0), axis=1, keepdims=True)
                gs_ref[slot, pl.ds(p0, SEG_ALIGN), :] = jnp.broadcast_to(gsum, (SEG_ALIGN, LANES))
            xs_ref[slot, pl.ds(r0, SORT_ROWS), :] = jnp.dot(
                sel_ref[u], xb, preferred_element_type=F32).astype(BF16)
        return carry

    lax.fori_loop(0, _sort_trips(tot_ref[i]), sort_rows, 0)

    @pl.when(i == nt - 1)
    def _():
        xs_ref[slot, 0:EXPERT_ROWS, :] = jnp.zeros((EXPERT_ROWS, D_MODEL), BF16)
        gs_ref[slot, 0:EXPERT_ROWS, :] = jnp.zeros((EXPERT_ROWS, LANES), F32)

    _segment_copies(i, *tables, copies(slot), lambda cp: cp.start())

    def unused_blocks(act):
        def per_block(j, carry):
            for cp in copies(slot)(0, pl.multiple_of(j * EXPERT_ROWS, EXPERT_ROWS), EXPERT_ROWS):
                act(cp)
            return carry
        lax.fori_loop(nused_ref[0], rows_hbm.shape[0] // EXPERT_ROWS, per_block, 0)

    @pl.when(i == nt - 1)
    def _():
        unused_blocks(lambda cp: cp.start())

        @pl.when(i >= 1)
        def _():
            _segment_copies(i - 1, *tables, copies(1 - slot), lambda cp: cp.wait())
        _segment_copies(i, *tables, copies(slot), lambda cp: cp.wait())
        unused_blocks(lambda cp: cp.wait())


def _dispatch(seg_len, seg_loc, seg_glob, tile_tot, n_used, grp_expert, x1b, pos_e, gate_e, n_rows):
    T = x1b.shape[0]
    ts = TOKEN_TILE
    nt = T // ts
    any_spec = pl.BlockSpec(memory_space=pl.ANY)

    def tile(i, *_):
        return jnp.minimum(i, nt - 1)

    grid_spec = pltpu.PrefetchScalarGridSpec(
        num_scalar_prefetch=6,
        grid=(nt + 1,),
        in_specs=[pl.BlockSpec((ts, D_MODEL), lambda i, *_: (tile(i), 0)),
                  pl.BlockSpec((N_EXPERTS, ts), lambda i, *_: (0, tile(i))),
                  pl.BlockSpec((N_EXPERTS, ts), lambda i, *_: (0, tile(i)))],
        out_specs=[any_spec, any_spec],
        scratch_shapes=[pltpu.VMEM((2, SORTED_ROWS, D_MODEL), BF16),
                        pltpu.VMEM((2, SORTED_ROWS, LANES), F32),
                        pltpu.VMEM((SORT_UNROLL, SORT_ROWS, ts), BF16),
                        pltpu.SemaphoreType.DMA((2, 2))],
    )
    return pl.pallas_call(
        _dispatch_kernel,
        grid_spec=grid_spec,
        out_shape=[jax.ShapeDtypeStruct((n_rows, D_MODEL), BF16), jax.ShapeDtypeStruct((n_rows, LANES), F32)],
        compiler_params=pltpu.CompilerParams(dimension_semantics=("arbitrary",),
                                             vmem_limit_bytes=VMEM_LIMIT),
        name="dispatch",
    )(seg_len, seg_loc, seg_glob, tile_tot, n_used, grp_expert, x1b, pos_e, gate_e)


def _combine_kernel(len_ref, loc_ref, glob_ref, tot_ref,
                    pos_ref, x1_ref, lng_ref, lnb_ref, y_hbm, o_ref, ybuf_ref, posb_ref, acc_ref, sem):
    i = pl.program_id(0)
    nt = pl.num_programs(0)
    slot = i % 2
    ts = x1_ref.shape[0]

    def copies(s):
        def make(lo, go, n):
            return (pltpu.make_async_copy(y_hbm.at[pl.ds(go, n)], ybuf_ref.at[s, pl.ds(lo, n)], sem.at[s]),)
        return make

    tables = (len_ref, loc_ref, glob_ref)

    @pl.when(i == 0)
    def _():
        ybuf_ref[...] = jnp.zeros(ybuf_ref.shape, ybuf_ref.dtype)
        _segment_copies(0, *tables, copies(0), lambda cp: cp.start())

    @pl.when(i + 1 < nt)
    def _():
        _segment_copies(i + 1, *tables, copies(1 - slot), lambda cp: cp.start())

    for k in range(TOP_K):
        posb_ref[k] = jnp.broadcast_to(pos_ref[:, k:k + 1], (ts, LANES))
    acc_ref[...] = jnp.zeros((ts, D_MODEL), F32)
    _segment_copies(i, *tables, copies(slot), lambda cp: cp.wait())
    lane = lax.broadcasted_iota(jnp.int32, (ts, LANES), 1)

    def gather_rows(c, carry):
        for u in range(SORT_UNROLL):
            r0 = pl.multiple_of((c * SORT_UNROLL + u) * SORT_ROWS, SORT_ROWS)
            halves = []
            for half in range(SORT_ROWS // LANES):
                col = lane + (r0 + half * LANES)
                hit = None
                for k in range(TOP_K):
                    hk = posb_ref[k] == col
                    hit = hk if hit is None else (hit | hk)
                halves.append(jnp.where(hit, 1.0, 0.0).astype(BF16))
            onehot = jnp.concatenate(halves, axis=1)
            acc_ref[...] += jnp.dot(onehot, ybuf_ref[slot, pl.ds(r0, SORT_ROWS), :],
                                    preferred_element_type=F32)
        return carry

    lax.fori_loop(0, _sort_trips(tot_ref[i]), gather_rows, 0)
    o_ref[...] = _layer_norm(DEEPNORM_ALPHA * x1_ref[...] + acc_ref[...], lng_ref[...], lnb_ref[...])


def _combine(seg_len, seg_loc, seg_glob, tile_tot, pos, x1, lng, lnb, y):
    T = x1.shape[0]
    ts = TOKEN_TILE
    grid_spec = pltpu.PrefetchScalarGridSpec(
        num_scalar_prefetch=4,
        grid=(T // ts,),
        in_specs=[pl.BlockSpec((ts, TOP_K), lambda i, *_: (i, 0)),
                  pl.BlockSpec((ts, D_MODEL), lambda i, *_: (i, 0)),
                  pl.BlockSpec((1, D_MODEL), lambda i, *_: (0, 0)),
                  pl.BlockSpec((1, D_MODEL), lambda i, *_: (0, 0)),
                  pl.BlockSpec(memory_space=pl.ANY)],
        out_specs=pl.BlockSpec((ts, D_MODEL), lambda i, *_: (i, 0)),
        scratch_shapes=[pltpu.VMEM((2, SORTED_ROWS, D_MODEL), BF16),
                        pltpu.VMEM((TOP_K, ts, LANES), jnp.int32),
                        pltpu.VMEM((ts, D_MODEL), F32),
                        pltpu.SemaphoreType.DMA((2,))],
    )
    return pl.pallas_call(
        _combine_kernel,
        grid_spec=grid_spec,
        out_shape=jax.ShapeDtypeStruct((T, D_MODEL), F32),
        compiler_params=pltpu.CompilerParams(dimension_semantics=("arbitrary",),
                                             vmem_limit_bytes=VMEM_LIMIT),
        name="combine",
    )(seg_len, seg_loc, seg_glob, tile_tot, pos, x1, lng, lnb, y)


def _moe(layer, x1, x1b, pos_t, pos_e, gate_e, cnt, wup, bup, wdn, bdn, lng, lnb):
    T = x1.shape[0]
    ts = TOKEN_TILE
    nt = T // ts
    r = EXPERT_ROWS
    n_blocks = -(-(T * TOP_K + nt * N_EXPERTS * (SEG_ALIGN - 1) + N_EXPERTS * (r - 1)) // r)
    counts = cnt[:, 0].reshape(nt, N_EXPERTS).astype(jnp.int32)
    seg_len = (counts + SEG_ALIGN - 1) // SEG_ALIGN * SEG_ALIGN
    seg_loc = jnp.cumsum(seg_len, axis=1) - seg_len
    tile_tot = jnp.sum(seg_len, axis=1)
    region = jnp.sum(seg_len, axis=0)
    region = (region + r - 1) // r * r
    region_end = jnp.cumsum(region)
    seg_glob = (region_end - region)[None, :] + jnp.cumsum(seg_len, axis=0) - seg_len
    n_used = (region_end[-1:] // r).astype(jnp.int32)
    block_start = jnp.arange(n_blocks, dtype=jnp.int32) * r
    block_expert = jnp.minimum(jnp.sum(region_end[None, :] <= block_start[:, None], axis=1),
                               N_EXPERTS - 1).astype(jnp.int32)
    used = jnp.sum(seg_len, axis=0)
    zero_row = jnp.zeros((1, N_EXPERTS), jnp.int32)
    tables = (jnp.concatenate([seg_len, (region - used)[None, :]]).reshape(-1),
              jnp.concatenate([seg_loc, zero_row]).reshape(-1),
              jnp.concatenate([seg_glob, (region_end - region + used)[None, :]]).reshape(-1),
              jnp.concatenate([tile_tot, jnp.zeros((1,), jnp.int32)]))
    group_start = jnp.arange(SORT_GROUPS, dtype=jnp.int32) * SEG_ALIGN
    seg_end = seg_loc + seg_len
    grp_expert = jnp.minimum(jnp.sum(seg_end[:, None, :] <= group_start[None, :, None], axis=2), N_EXPERTS - 1)
    grp_expert = jnp.concatenate([grp_expert, jnp.zeros((1, SORT_GROUPS), jnp.int32)]).astype(jnp.int32)
    rows, row_gate = _dispatch(*tables, n_used, grp_expert.reshape(-1), x1b, pos_e, gate_e, n_blocks * r)
    block_rows = jnp.clip((region_end - region + used)[block_expert] - block_start, 0, r).astype(jnp.int32)
    y = _experts(layer, block_expert, n_used, block_rows, rows, row_gate, wup, bup, wdn, bdn)
    return _combine(*tables, pos_t.T, x1, lng, lnb, y)


def kernel(x, mem, positions, a_w_in, a_b_in, a_dw, a_dw_b, a_cn_g, a_cn_b, a_w_out, a_b_out, w_kv_shared, b_w_q, b_w_out, b_b_out, mem_w_kv, ln_g, ln_b, router_w, router_b, exp_w_up, exp_b_up, exp_w_down, exp_b_down):
    B, S, D = x.shape
    T = B * S
    assert D == D_MODEL and S % DIL_TILE == 0 and S % TOKEN_TILE == 0
    xt = x.reshape(T, D)
    kst, vst = _mem_kv(mem.astype(BF16), mem_w_kv.astype(BF16))
    pos = positions.reshape(T, 1).astype(jnp.int32)
    half = ROT_DIM // 2
    inv_freq = jnp.power(ROPE_THETA, -jnp.arange(half, dtype=F32) / half)
    e = jnp.arange(LANES) % HEAD_DIM
    freq = jnp.where(e < ROT_DIM, inv_freq[e % half], 0.0).reshape(1, LANES).astype(F32)
    dw_pad = jnp.pad(a_dw, ((0, 0), (0, 32 - CONV_WIDTH), (0, 0)))
    b_up = exp_b_up.reshape(DEPTH, N_EXPERTS, 1, 2 * D_FF)
    b_down = exp_b_down.reshape(DEPTH, N_EXPERTS, 1, D_MODEL)
    shared_kv = None
    for l in range(DEPTH):
        wrt = router_w[l].T
        br = router_b[l].reshape(N_EXPERTS, 1)
        lng0, lnb0 = _row(ln_g[l, 0]), _row(ln_b[l, 0])
        if l < N_A_LAYERS:
            outs = _layer_a(xt, S, a_w_in[l].astype(BF16), _row(a_b_in[l]), dw_pad[l], _row(a_dw_b[l]),
                            _row(a_cn_g[l]), _row(a_cn_b[l]), kst[l], vst[l],
                            a_w_out[l].astype(BF16), _row(a_b_out[l]), lng0, lnb0, wrt, br)
        else:
            j = l - N_A_LAYERS
            if j == 0:
                q, qm, k, v = _layer_b_proj(xt, B, S, pos, freq, b_w_q[j].astype(BF16), w_kv_shared.astype(BF16))
                shared_kv = (k, v)
            else:
                q, qm = _layer_b_proj(xt, B, S, pos, freq, b_w_q[j].astype(BF16), None)
            dil = []
            for g, (window, dilation) in enumerate(DIL_GROUPS):
                assert window // dilation == DIL_BLOCK and DIL_TILE % (DIL_BLOCK * dilation) == 0
                dil += _dilated_group(g, dilation, q, shared_kv[0], shared_kv[1])
            outs = _layer_b_out(xt, S, dil, qm, kst[l], vst[l], b_w_out[j].astype(BF16), _row(b_b_out[j]),
                                lng0, lnb0, wrt, br)
        x1, x1b, pos_t, pos_e, gate_e, cnt = outs
        xt = _moe(l, x1, x1b, pos_t, pos_e, gate_e, cnt, exp_w_up, b_up, exp_w_down, b_down,
                  _row(ln_g[l, 1]), _row(ln_b[l, 1]))
    return xt.reshape(B, S, D)
```

```python
import functools

import jax
import jax.numpy as jnp
from jax import lax
from jax.experimental import pallas as pl
from jax.experimental.pallas import tpu as pltpu

D_MODEL = 1024
DEPTH = 4
N_A_LAYERS = 2
HEAD_DIM = 64
ROT_DIM = 16
ROPE_THETA = 500000.0
CONV_CH = 768
CONV_WIDTH = 31
MEM_TOKENS = 256
MEM_WIDTH = 256
DIL_GROUPS = ((128, 1), (512, 4), (2048, 16))
DIL_WIDTH = 256
DIL_BLOCK = 128
N_EXPERTS = 32
TOP_K = 4
D_FF = 1024
SWIGLU_ALPHA = 1.702
SWIGLU_LIMIT = 7.0
DEEPNORM_ALPHA = (2 * DEPTH) ** 0.25
LN_EPS = 1e-5
ATTN_SCALE = HEAD_DIM ** -0.5

LANES = 128
TOKEN_TILE = 512
DIL_TILE = 2048
EXPERT_ROWS = 512
CONV_ROWS = 64
CONV_LANES = 384
SEG_ALIGN = 16
SEG_BIG = 128
SEG_BITS = (64, 32, 16)
SORT_ROWS = 256
SORT_UNROLL = 5
SORTED_ROWS = 2560
SORT_GROUPS = SORTED_ROWS // SEG_ALIGN
VMEM_LIMIT = 56 * 1024 * 1024

F32 = jnp.float32
BF16 = jnp.bfloat16
NEG = -1e30


def _layer_norm(v, g, b):
    mu = jnp.mean(v, axis=-1, keepdims=True)
    c = v - mu
    var = jnp.mean(c * c, axis=-1, keepdims=True)
    return c * lax.rsqrt(var + LN_EPS) * g + b


def _nt_dot(a, b, **kw):
    return lax.dot_general(a, b, (((1,), (1,)), ((), ())), preferred_element_type=F32, **kw)


def _mem_attention(qm_bf16, kst_ref, vst_ref):
    out = None
    for h in range(4):
        s = _nt_dot(qm_bf16, kst_ref[h])
        m = jnp.max(s, axis=-1, keepdims=True)
        p = jnp.exp(s - m)
        l = jnp.sum(p, axis=-1, keepdims=True)
        pn = (p * (1.0 / l)).astype(BF16)
        u = jnp.dot(pn, vst_ref[h], preferred_element_type=F32)
        out = u if out is None else out + u
    return out


def _residual_ln_router(x_res, y, lng_ref, lnb_ref, wrt_ref, br_ref,
                        x1_ref, x1b_ref, pos_ref, pose_ref, gatee_ref, cnt_ref):
    ts = x_res.shape[0]
    x1 = _layer_norm(DEEPNORM_ALPHA * x_res + y, lng_ref[...], lnb_ref[...])
    x1_ref[...] = x1
    x1b_ref[...] = x1.astype(BF16)
    logit = _nt_dot(wrt_ref[...], x1, precision=lax.Precision.HIGHEST) + br_ref[...]
    row = lax.broadcasted_iota(jnp.int32, (N_EXPERTS, ts), 0)
    vals, onehots = [], []
    for k in range(TOP_K):
        m = jnp.max(logit, axis=0, keepdims=True)
        idx = jnp.min(jnp.where(logit == m, row, N_EXPERTS), axis=0, keepdims=True)
        oh = row == idx
        logit = jnp.where(oh, -jnp.inf, logit)
        vals.append(m)
        onehots.append(oh)
    exps = [jnp.exp(v - vals[0]) for v in vals]
    inv = 1.0 / (exps[0] + exps[1] + exps[2] + exps[3])
    gate_e = jnp.zeros((N_EXPERTS, ts), F32)
    for k in range(TOP_K):
        gate_e = jnp.where(onehots[k], exps[k] * inv, gate_e)
    gatee_ref[...] = gate_e
    oh_all = (onehots[0] | onehots[1] | onehots[2] | onehots[3])
    oh_f = jnp.where(oh_all, 1.0, 0.0)
    ri = lax.broadcasted_iota(jnp.int32, (ts, ts), 0)
    ci = lax.broadcasted_iota(jnp.int32, (ts, ts), 1)
    tri = jnp.where(ri < ci, 1.0, 0.0).astype(BF16)
    cum = jnp.dot(oh_f.astype(BF16), tri, preferred_element_type=F32)
    cnt = jnp.sum(oh_f, axis=1, keepdims=True)
    cnt_b = jnp.broadcast_to(cnt, (N_EXPERTS, LANES))
    cnt_ref[...] = cnt_b
    units = jnp.floor((cnt_b + (SEG_ALIGN - 1)) * (1.0 / SEG_ALIGN))
    er = lax.broadcasted_iota(jnp.int32, (N_EXPERTS, N_EXPERTS), 0)
    ec = lax.broadcasted_iota(jnp.int32, (N_EXPERTS, N_EXPERTS), 1)
    low_tri = jnp.where(ec < er, 1.0, 0.0).astype(BF16)
    seg_start = jnp.dot(low_tri, units.astype(BF16), preferred_element_type=F32)[:, 0:1] * SEG_ALIGN
    pos_all = cum + seg_start
    pose_ref[...] = jnp.where(oh_all, pos_all, -1.0).astype(jnp.int32)
    for k in range(TOP_K):
        rk = jnp.sum(jnp.where(onehots[k], pos_all, 0.0), axis=0, keepdims=True)
        pos_ref[k:k + 1, :] = rk.astype(jnp.int32)


def _mem_kv_kernel(mem_ref, w_ref, kst_ref, vst_ref):
    kv = jnp.dot(mem_ref[...], w_ref[...], preferred_element_type=F32)
    k = kv[:, :MEM_WIDTH] * ATTN_SCALE
    v = kv[:, MEM_WIDTH:]
    head = lax.broadcasted_iota(jnp.int32, (MEM_TOKENS, MEM_WIDTH), 1) // HEAD_DIM
    for h in range(4):
        kst_ref[h] = jnp.where(head == h, k, 0.0).astype(BF16)
        vst_ref[h] = jnp.where(head == h, v, 0.0).astype(BF16)


def _mem_kv(mem_b, w_b):
    B = mem_b.shape[0]
    out = jax.ShapeDtypeStruct((DEPTH, B, 4, MEM_TOKENS, MEM_WIDTH), BF16)
    return pl.pallas_call(
        _mem_kv_kernel,
        grid=(DEPTH, B),
        in_specs=[pl.BlockSpec((None, MEM_TOKENS, D_MODEL), lambda l, b: (b, 0, 0)),
                  pl.BlockSpec((None, D_MODEL, 2 * MEM_WIDTH), lambda l, b: (l, 0, 0))],
        out_specs=[pl.BlockSpec((None, None, 4, MEM_TOKENS, MEM_WIDTH), lambda l, b: (l, b, 0, 0, 0))] * 2,
        out_shape=[out, out],
        name="mem_kv",
    )(mem_b, w_b)


def _layer_a_kernel(tiles_per_batch,
                    x_ref, win_ref, bin_ref, dw_ref, dwb_ref, cng_ref, cnb_ref, kst_ref, vst_ref,
                    wout_ref, bout_ref, lng_ref, lnb_ref, wrt_ref, br_ref,
                    x1_ref, x1b_ref, pos_ref, pose_ref, gatee_ref, cnt_ref,
                    hpad_ref, shift_ref, cat_ref):
    ts = x_ref.shape[0]
    i = pl.program_id(0)
    x = x_ref[...]
    h = jnp.dot(x.astype(BF16), win_ref[...], preferred_element_type=F32) + bin_ref[...]
    hg = h[:, :CONV_CH] * jax.nn.sigmoid(h[:, CONV_CH:2 * CONV_CH])
    qm = h[:, 2 * CONV_CH:].astype(BF16)

    @pl.when(i % tiles_per_batch == 0)
    def _():
        hpad_ref[0:32, :] = jnp.zeros((32, CONV_CH), F32)

    hpad_ref[32:32 + ts, :] = hg
    for b in range(8):
        n = ts + 8 * ((CONV_WIDTH - 1 - b) // 8)
        for r0 in range(0, n, 128):
            rows = min(128, n - r0)
            shift_ref[b, r0:r0 + rows, :] = hpad_ref[pl.ds(2 + b + r0, rows), :]
    hpad_ref[0:32, :] = hpad_ref[ts:ts + 32, :]

    for c0 in range(0, CONV_CH, CONV_LANES):
        lanes = slice(c0, c0 + CONV_LANES)

        def conv_rows(c, carry, lanes=lanes):
            r0 = pl.multiple_of(c * CONV_ROWS, CONV_ROWS)
            acc = jnp.zeros((CONV_ROWS, CONV_LANES), F32) + dwb_ref[:, lanes]
            for k in range(CONV_WIDTH):
                a, b = divmod(k, 8)
                acc = acc + shift_ref[b, pl.ds(r0 + 8 * a, CONV_ROWS), lanes] * dw_ref[k:k + 1, lanes]
            hpad_ref[pl.ds(32 + r0, CONV_ROWS), lanes] = acc
            return carry

        lax.fori_loop(0, ts // CONV_ROWS, conv_rows, 0)

    def norm_rows(c, carry):
        r0 = pl.multiple_of(c * CONV_ROWS, CONV_ROWS)
        cn = _layer_norm(hpad_ref[pl.ds(32 + r0, CONV_ROWS), :], cng_ref[...], cnb_ref[...])
        cat_ref[pl.ds(r0, CONV_ROWS), 0:CONV_CH] = (cn * jax.nn.sigmoid(cn)).astype(BF16)
        return carry

    lax.fori_loop(0, ts // CONV_ROWS, norm_rows, 0)

    cat_ref[:, CONV_CH:] = _mem_attention(qm, kst_ref, vst_ref).astype(BF16)
    y = jnp.dot(cat_ref[...], wout_ref[...], preferred_element_type=F32) + bout_ref[...]
    _residual_ln_router(x, y, lng_ref, lnb_ref, wrt_ref, br_ref,
                        x1_ref, x1b_ref, pos_ref, pose_ref, gatee_ref, cnt_ref)


def _row(v):
    return v.reshape(1, -1).astype(F32)


def _const_spec(shape):
    nd = len(shape)
    return pl.BlockSpec(shape, lambda i: (0,) * nd)


def _router_out(T, ts):
    nt = T // ts
    shapes = [jax.ShapeDtypeStruct((T, D_MODEL), F32), jax.ShapeDtypeStruct((T, D_MODEL), BF16),
              jax.ShapeDtypeStruct((TOP_K, T), jnp.int32), jax.ShapeDtypeStruct((N_EXPERTS, T), jnp.int32),
              jax.ShapeDtypeStruct((N_EXPERTS, T), F32), jax.ShapeDtypeStruct((nt * N_EXPERTS, LANES), F32)]
    specs = [pl.BlockSpec((ts, D_MODEL), lambda i: (i, 0)), pl.BlockSpec((ts, D_MODEL), lambda i: (i, 0)),
             pl.BlockSpec((TOP_K, ts), lambda i: (0, i)), pl.BlockSpec((N_EXPERTS, ts), lambda i: (0, i)),
             pl.BlockSpec((N_EXPERTS, ts), lambda i: (0, i)), pl.BlockSpec((N_EXPERTS, LANES), lambda i: (i, 0))]
    return shapes, specs


def _layer_a(x, S, win, bin_, dw, dwb, cng, cnb, kst, vst, wout, bout, lng, lnb, wrt, br):
    T = x.shape[0]
    ts = TOKEN_TILE
    tpb = S // ts
    out_shapes, out_specs = _router_out(T, ts)
    kv_spec = pl.BlockSpec((None, 4, MEM_TOKENS, MEM_WIDTH), lambda i: (i // tpb, 0, 0, 0))
    a_in = 2 * CONV_CH + MEM_WIDTH
    return pl.pallas_call(
        functools.partial(_layer_a_kernel, tpb),
        grid=(T // ts,),
        in_specs=[pl.BlockSpec((ts, D_MODEL), lambda i: (i, 0)),
                  _const_spec((D_MODEL, a_in)), _const_spec((1, a_in)),
                  _const_spec((32, CONV_CH)), _const_spec((1, CONV_CH)),
                  _const_spec((1, CONV_CH)), _const_spec((1, CONV_CH)),
                  kv_spec, kv_spec,
                  _const_spec((D_MODEL, D_MODEL)), _const_spec((1, D_MODEL)),
                  _const_spec((1, D_MODEL)), _const_spec((1, D_MODEL)),
                  _const_spec((N_EXPERTS, D_MODEL)), _const_spec((N_EXPERTS, 1))],
        out_specs=out_specs,
        out_shape=out_shapes,
        scratch_shapes=[pltpu.VMEM((ts + 32, CONV_CH), F32),
                        pltpu.VMEM((8, ts + 24, CONV_CH), F32),
                        pltpu.VMEM((ts, D_MODEL), BF16)],
        compiler_params=pltpu.CompilerParams(dimension_semantics=("arbitrary",),
                                             vmem_limit_bytes=VMEM_LIMIT),
        name="layer_a",
    )(x, win, bin_, dw, dwb, cng, cnb, kst, vst, wout, bout, lng, lnb, wrt, br)


def _rotary(v, cos_t, sin_t, low):
    partner = jnp.where(low, pltpu.roll(v, LANES - 8, axis=1), pltpu.roll(v, 8, axis=1))
    return v * cos_t + partner * sin_t


def _rope_tables(pos_ref, freq_ref):
    e = lax.broadcasted_iota(jnp.int32, (1, LANES), 1) % HEAD_DIM
    ang = pos_ref[...].astype(F32) * freq_ref[...]
    low = e < ROT_DIM // 2
    rot = e < ROT_DIM
    cos_t = jnp.where(rot, jnp.cos(ang), 1.0)
    sn = jnp.sin(ang)
    sin_t = jnp.where(low, -sn, jnp.where(rot, sn, 0.0))
    return cos_t, sin_t, low


def _layer_b_proj_kernel(with_kv, x_ref, pos_ref, freq_ref, wq_ref, *rest):
    if with_kv:
        wkv_ref, q_ref, qm_ref, k_ref, v_ref = rest
    else:
        q_ref, qm_ref = rest
    xb = x_ref[...].astype(BF16)
    cos_t, sin_t, low = _rope_tables(pos_ref, freq_ref)
    q = jnp.dot(xb, wq_ref[...], preferred_element_type=F32)
    for g in range(3):
        for s in range(2):
            c0 = g * DIL_WIDTH + s * LANES
            q_ref[g, s] = _rotary(q[:, c0:c0 + LANES], cos_t, sin_t, low) * ATTN_SCALE
    qm_ref[...] = q[:, 3 * DIL_WIDTH:].astype(BF16)
    if with_kv:
        kv = jnp.dot(xb, wkv_ref[...], preferred_element_type=F32)
        for g in range(3):
            for s in range(2):
                c0 = g * 2 * DIL_WIDTH + s * LANES
                k_ref[g, s] = _rotary(kv[:, c0:c0 + LANES], cos_t, sin_t, low)
                v_ref[g, s] = kv[:, c0 + DIL_WIDTH:c0 + DIL_WIDTH + LANES]


def _layer_b_proj(x, B, S, pos, freq, wq, wkv):
    T = x.shape[0]
    ts = TOKEN_TILE
    tpb = S // ts
    with_kv = wkv is not None
    slab = jax.ShapeDtypeStruct((3, B, 2, S, LANES), F32)
    slab_spec = pl.BlockSpec((3, None, 2, ts, LANES), lambda i: (0, i // tpb, 0, i % tpb, 0))
    in_specs = [pl.BlockSpec((ts, D_MODEL), lambda i: (i, 0)),
                pl.BlockSpec((ts, 1), lambda i: (i, 0)),
                _const_spec((1, LANES)),
                _const_spec((D_MODEL, D_MODEL))]
    args = [x, pos, freq, wq]
    out_shape = [slab, jax.ShapeDtypeStruct((T, MEM_WIDTH), BF16)]
    out_specs = [slab_spec, pl.BlockSpec((ts, MEM_WIDTH), lambda i: (i, 0))]
    if with_kv:
        in_specs.append(_const_spec((D_MODEL, 6 * DIL_WIDTH)))
        args.append(wkv)
        out_shape += [slab, slab]
        out_specs += [slab_spec, slab_spec]
    return pl.pallas_call(
        functools.partial(_layer_b_proj_kernel, with_kv),
        grid=(T // ts,),
        in_specs=in_specs, out_specs=out_specs, out_shape=out_shape,
        compiler_params=pltpu.CompilerParams(dimension_semantics=("arbitrary",),
                                             vmem_limit_bytes=VMEM_LIMIT),
        name="layer_b_proj_kv" if with_kv else "layer_b_proj",
    )(*args)


def _dilated_kernel(d, q_ref, k_ref, v_ref, o_ref, lse_ref, kbuf_ref, vbuf_ref):
    i = pl.program_id(1)
    tile = DIL_TILE
    blk = DIL_BLOCK

    @pl.when(i == 0)
    def _():
        kbuf_ref[:, 0:tile, :] = jnp.zeros((2, tile, LANES), F32)
        vbuf_ref[:, 0:tile, :] = jnp.zeros((2, tile, LANES), F32)

    kbuf_ref[:, tile:, :] = k_ref[...]
    vbuf_ref[:, tile:, :] = v_ref[...]

    def rows(start, n):
        return pl.ds(start, n) if d == 1 else pl.ds(start, n, stride=d)

    qi = lax.broadcasted_iota(jnp.int32, (blk, 2 * blk), 0)
    ki = lax.broadcasted_iota(jnp.int32, (blk, 2 * blk), 1)
    dist = qi + blk - ki
    band = (dist >= 0) & (dist <= blk)
    head = lax.broadcasted_iota(jnp.int32, (1, LANES), 1) // HEAD_DIM

    def block(j, carry):
        c = j // d
        r = j % d
        qs = c * (blk * d) + r
        ks = tile + qs - blk * d
        first = jnp.logical_and(i == 0, c == 0)
        valid = band & jnp.logical_or(jnp.logical_not(first), ki >= blk)
        for s in range(2):
            q = q_ref[s, rows(qs, blk), :].astype(BF16)
            kk = kbuf_ref[s, rows(ks, 2 * blk), :]
            vv = vbuf_ref[s, rows(ks, 2 * blk), :]
            o_acc = jnp.zeros((blk, LANES), F32)
            lse_acc = jnp.zeros((blk, LANES), F32)
            for h in range(2):
                hm = head == h
                sc = _nt_dot(q, jnp.where(hm, kk, 0.0).astype(BF16))
                sc = jnp.where(valid, sc, NEG)
                m = jnp.max(sc, axis=-1, keepdims=True)
                p = jnp.exp(sc - m)
                l = jnp.sum(p, axis=-1, keepdims=True)
                u = jnp.dot(p.astype(BF16), jnp.where(hm, vv, 0.0).astype(BF16),
                            preferred_element_type=F32)
                o_acc = o_acc + u * (1.0 / l)
                lse_acc = jnp.where(hm, m + jnp.log(l), lse_acc)
            o_ref[s, rows(qs, blk), :] = o_acc
            lse_ref[s, rows(qs, blk), :] = lse_acc
        return carry

    lax.fori_loop(0, tile // blk, block, 0, unroll=4)
    kbuf_ref[:, 0:tile, :] = kbuf_ref[:, tile:, :]
    vbuf_ref[:, 0:tile, :] = vbuf_ref[:, tile:, :]


def _dilated_group(g, d, q, k, v):
    _, B, _, S, _ = q.shape
    tile = DIL_TILE
    in_spec = pl.BlockSpec((None, None, 2, tile, LANES), lambda b, i: (g, b, 0, i, 0))
    out_spec = pl.BlockSpec((None, 2, tile, LANES), lambda b, i: (b, 0, i, 0))
    out = jax.ShapeDtypeStruct((B, 2, S, LANES), F32)
    return pl.pallas_call(
        functools.partial(_dilated_kernel, d),
        grid=(B, S // tile),
        in_specs=[in_spec, in_spec, in_spec],
        out_specs=[out_spec, out_spec],
        out_shape=[out, out],
        scratch_shapes=[pltpu.VMEM((2, 2 * tile, LANES), F32), pltpu.VMEM((2, 2 * tile, LANES), F32)],
        compiler_params=pltpu.CompilerParams(dimension_semantics=("arbitrary", "arbitrary"),
                                             vmem_limit_bytes=VMEM_LIMIT),
        name=f"dilated_d{d}",
    )(q, k, v)


def _layer_b_out_kernel(x_ref, o0_ref, l0_ref, o1_ref, l1_ref, o2_ref, l2_ref, qm_ref, kst_ref, vst_ref,
                        wout_ref, bout_ref, lng_ref, lnb_ref, wrt_ref, br_ref,
                        x1_ref, x1b_ref, pos_ref, pose_ref, gatee_ref, cnt_ref, cat_ref):
    for s in range(2):
        lses = [l0_ref[s], l1_ref[s], l2_ref[s]]
        outs = [o0_ref[s], o1_ref[s], o2_ref[s]]
        m = jnp.maximum(jnp.maximum(lses[0], lses[1]), lses[2])
        es = [jnp.exp(l - m) for l in lses]
        inv = 1.0 / (es[0] + es[1] + es[2])
        dil = (es[0] * inv) * outs[0] + (es[1] * inv) * outs[1] + (es[2] * inv) * outs[2]
        cat_ref[:, s * LANES:(s + 1) * LANES] = dil.astype(BF16)
    cat_ref[:, DIL_WIDTH:] = _mem_attention(qm_ref[...], kst_ref, vst_ref).astype(BF16)
    y = jnp.dot(cat_ref[...], wout_ref[...], preferred_element_type=F32) + bout_ref[...]
    _residual_ln_router(x_ref[...], y, lng_ref, lnb_ref, wrt_ref, br_ref,
                        x1_ref, x1b_ref, pos_ref, pose_ref, gatee_ref, cnt_ref)


def _layer_b_out(x, S, dil, qm, kst, vst, wout, bout, lng, lnb, wrt, br):
    T = x.shape[0]
    ts = TOKEN_TILE
    tpb = S // ts
    out_shapes, out_specs = _router_out(T, ts)
    kv_spec = pl.BlockSpec((None, 4, MEM_TOKENS, MEM_WIDTH), lambda i: (i // tpb, 0, 0, 0))
    slab_spec = pl.BlockSpec((None, 2, ts, LANES), lambda i: (i // tpb, 0, i % tpb, 0))
    b_out = DIL_WIDTH + MEM_WIDTH
    return pl.pallas_call(
        _layer_b_out_kernel,
        grid=(T // ts,),
        in_specs=[pl.BlockSpec((ts, D_MODEL), lambda i: (i, 0))] + [slab_spec] * 6 + [
            pl.BlockSpec((ts, MEM_WIDTH), lambda i: (i, 0)), kv_spec, kv_spec,
            _const_spec((b_out, D_MODEL)), _const_spec((1, D_MODEL)),
            _const_spec((1, D_MODEL)), _const_spec((1, D_MODEL)),
            _const_spec((N_EXPERTS, D_MODEL)), _const_spec((N_EXPERTS, 1))],
        out_specs=out_specs,
        out_shape=out_shapes,
        scratch_shapes=[pltpu.VMEM((ts, b_out), BF16)],
        compiler_params=pltpu.CompilerParams(dimension_semantics=("arbitrary",),
                                             vmem_limit_bytes=VMEM_LIMIT),
        name="layer_b_out",
    )(x, *dil, qm, kst, vst, wout, bout, lng, lnb, wrt, br)


def _used_block(i, nu):
    return jnp.maximum(jnp.minimum(i, nu[0] - 1), 0)


def _expert_kernel(be_ref, nused_ref, valid_ref, rows_ref, gate_ref, wup_ref, bup_ref, wdn_ref, bdn_ref, y_ref,
                   wup_bf_ref, wdn_bf_ref):
    i = pl.program_id(0)
    used = i < nused_ref[0]
    half = rows_ref.shape[0] // 2

    @pl.when(jnp.logical_not(used))
    def _():
        y_ref[...] = jnp.zeros(y_ref.shape, y_ref.dtype)

    b = _used_block(i, nused_ref)
    new_expert = jnp.logical_or(i == 0, be_ref[b] != be_ref[jnp.maximum(b - 1, 0)])

    @pl.when(jnp.logical_and(used, new_expert))
    def _():
        wup_bf_ref[...] = wup_ref[...].astype(BF16)
        wdn_bf_ref[...] = wdn_ref[...].astype(BF16)

    def ffn(rs):
        h = jnp.dot(rows_ref[rs, :], wup_bf_ref[...], preferred_element_type=F32) + bup_ref[...]
        g = jnp.minimum(h[:, :D_FF], SWIGLU_LIMIT)
        lin = jnp.clip(h[:, D_FF:], -SWIGLU_LIMIT, SWIGLU_LIMIT)
        act = g * jax.nn.sigmoid(SWIGLU_ALPHA * g) * (lin + 1.0)
        y = jnp.dot(act.astype(BF16), wdn_bf_ref[...], preferred_element_type=F32) + bdn_ref[...]
        gate = gate_ref[rs, :]
        for c in range(D_MODEL // LANES):
            y_ref[rs, c * LANES:(c + 1) * LANES] = (
                y[:, c * LANES:(c + 1) * LANES] * gate).astype(y_ref.dtype)

    data_rows = valid_ref[b]

    @pl.when(jnp.logical_and(used, data_rows > half))
    def _():
        ffn(slice(None))

    @pl.when(jnp.logical_and(used, data_rows <= half))
    def _():
        ffn(slice(0, half))
        y_ref[half:, :] = jnp.zeros((y_ref.shape[0] - half, D_MODEL), y_ref.dtype)


def _experts(layer, block_expert, n_used, block_rows, rows, row_gate, wup, bup, wdn, bdn):
    nr = rows.shape[0]
    r = EXPERT_ROWS

    def row_map(i, be, nu, br):
        return (_used_block(i, nu), 0)

    def wmap(i, be, nu, br):
        return (layer, be[_used_block(i, nu)], 0, 0)

    grid_spec = pltpu.PrefetchScalarGridSpec(
        num_scalar_prefetch=3,
        grid=(nr // r,),
        in_specs=[pl.BlockSpec((r, D_MODEL), row_map),
                  pl.BlockSpec((r, LANES), row_map),
                  pl.BlockSpec((None, None, D_MODEL, 2 * D_FF), wmap),
                  pl.BlockSpec((None, None, 1, 2 * D_FF), wmap),
                  pl.BlockSpec((None, None, D_FF, D_MODEL), wmap),
                  pl.BlockSpec((None, None, 1, D_MODEL), wmap)],
        out_specs=pl.BlockSpec((r, D_MODEL), lambda i, be, nu, br: (i, 0)),
        scratch_shapes=[pltpu.VMEM((D_MODEL, 2 * D_FF), BF16), pltpu.VMEM((D_FF, D_MODEL), BF16)],
    )
    return pl.pallas_call(
        _expert_kernel,
        grid_spec=grid_spec,
        out_shape=jax.ShapeDtypeStruct((nr, D_MODEL), BF16),
        compiler_params=pltpu.CompilerParams(dimension_semantics=("arbitrary",),
                                             vmem_limit_bytes=VMEM_LIMIT),
        name="experts",
    )(block_expert, n_used, block_rows, rows, row_gate, wup, bup, wdn, bdn)


def _sort_trips(n_rows):
    step = SORT_ROWS * SORT_UNROLL
    return (n_rows + step - 1) // step


def _segment_copies(tile, len_ref, loc_ref, glob_ref, make_copies, act):
    def per_expert(e, carry):
        n = len_ref[tile * N_EXPERTS + e]
        loc = loc_ref[tile * N_EXPERTS + e]
        glob = glob_ref[tile * N_EXPERTS + e]

        def big_piece(j, c):
            lo = pl.multiple_of(loc + j * SEG_BIG, SEG_ALIGN)
            go = pl.multiple_of(glob + j * SEG_BIG, SEG_ALIGN)
            for cp in make_copies(lo, go, SEG_BIG):
                act(cp)
            return c

        lax.fori_loop(0, n // SEG_BIG, big_piece, 0)
        off = n // SEG_BIG * SEG_BIG
        for bit in SEG_BITS:
            part = n & bit

            @pl.when(part != 0)
            def _():
                lo = pl.multiple_of(loc + off, SEG_ALIGN)
                go = pl.multiple_of(glob + off, SEG_ALIGN)
                for cp in make_copies(lo, go, bit):
                    act(cp)

            off = off + part
        return carry

    lax.fori_loop(0, N_EXPERTS, per_expert, 0)


def _dispatch_kernel(len_ref, loc_ref, glob_ref, tot_ref, nused_ref, grp_ref,
                     xb_ref, pose_ref, gatee_ref,
                     rows_hbm, gates_hbm, xs_ref, gs_ref, sel_ref, sem):
    i = pl.program_id(0)
    nt = pl.num_programs(0)
    slot = i % 2
    ts = xb_ref.shape[0]

    def copies(s):
        def make(lo, go, n):
            return (pltpu.make_async_copy(xs_ref.at[s, pl.ds(lo, n)], rows_hbm.at[pl.ds(go, n)], sem.at[0, s]),
                    pltpu.make_async_copy(gs_ref.at[s, pl.ds(lo, n)], gates_hbm.at[pl.ds(go, n)], sem.at[1, s]))
        return make

    tables = (len_ref, loc_ref, glob_ref)

    @pl.when(i >= 2)
    def _():
        _segment_copies(i - 2, *tables, copies(slot), lambda cp: cp.wait())

    xb = xb_ref[...]
    prow = lax.broadcasted_iota(jnp.int32, (SEG_ALIGN, ts), 0)

    def sort_rows(c, carry):
        for u in range(SORT_UNROLL):
            chunk = c * SORT_UNROLL + u
            r0 = pl.multiple_of(chunk * SORT_ROWS, SORT_ROWS)
            for g in range(SORT_ROWS // SEG_ALIGN):
                e = grp_ref[i * SORT_GROUPS + chunk * (SORT_ROWS // SEG_ALIGN) + g]
                p0 = r0 + g * SEG_ALIGN
                hit = (prow + p0) == pose_ref[pl.ds(e, 1), :]
                sel_ref[u, g * SEG_ALIGN:(g + 1) * SEG_ALIGN, :] = jnp.where(hit, 1.0, 0.0).astype(BF16)
                gsum = jnp.sum(jnp.where(hit, gatee_ref[pl.ds(e, 1), :], 0.0), axis=1, keepdims=True)
                gs_ref[slot, pl.ds(p0, SEG_ALIGN), :] = jnp.broadcast_to(gsum, (SEG_ALIGN, LANES))
            xs_ref[slot, pl.ds(r0, SORT_ROWS), :] = jnp.dot(
                sel_ref[u], xb, preferred_element_type=F32).astype(BF16)
        return carry

    lax.fori_loop(0, _sort_trips(tot_ref[i]), sort_rows, 0)

    @pl.when(i == nt - 1)
    def _():
        xs_ref[slot, 0:EXPERT_ROWS, :] = jnp.zeros((EXPERT_ROWS, D_MODEL), BF16)
        gs_ref[slot, 0:EXPERT_ROWS, :] = jnp.zeros((EXPERT_ROWS, LANES), F32)

    _segment_copies(i, *tables, copies(slot), lambda cp: cp.start())

    def unused_blocks(act):
        def per_block(j, carry):
            for cp in copies(slot)(0, pl.multiple_of(j * EXPERT_ROWS, EXPERT_ROWS), EXPERT_ROWS):
                act(cp)
            return carry
        lax.fori_loop(nused_ref[0], rows_hbm.shape[0] // EXPERT_ROWS, per_block, 0)

    @pl.when(i == nt - 1)
    def _():
        unused_blocks(lambda cp: cp.start())

        @pl.when(i >= 1)
        def _():
            _segment_copies(i - 1, *tables, copies(1 - slot), lambda cp: cp.wait())
        _segment_copies(i, *tables, copies(slot), lambda cp: cp.wait())
        unused_blocks(lambda cp: cp.wait())


def _dispatch(seg_len, seg_loc, seg_glob, tile_tot, n_used, grp_expert, x1b, pos_e, gate_e, n_rows):
    T = x1b.shape[0]
    ts = TOKEN_TILE
    nt = T // ts
    any_spec = pl.BlockSpec(memory_space=pl.ANY)

    def tile(i, *_):
        return jnp.minimum(i, nt - 1)

    grid_spec = pltpu.PrefetchScalarGridSpec(
        num_scalar_prefetch=6,
        grid=(nt + 1,),
        in_specs=[pl.BlockSpec((ts, D_MODEL), lambda i, *_: (tile(i), 0)),
                  pl.BlockSpec((N_EXPERTS, ts), lambda i, *_: (0, tile(i))),
                  pl.BlockSpec((N_EXPERTS, ts), lambda i, *_: (0, tile(i)))],
        out_specs=[any_spec, any_spec],
        scratch_shapes=[pltpu.VMEM((2, SORTED_ROWS, D_MODEL), BF16),
                        pltpu.VMEM((2, SORTED_ROWS, LANES), F32),
                        pltpu.VMEM((SORT_UNROLL, SORT_ROWS, ts), BF16),
                        pltpu.SemaphoreType.DMA((2, 2))],
    )
    return pl.pallas_call(
        _dispatch_kernel,
        grid_spec=grid_spec,
        out_shape=[jax.ShapeDtypeStruct((n_rows, D_MODEL), BF16), jax.ShapeDtypeStruct((n_rows, LANES), F32)],
        compiler_params=pltpu.CompilerParams(dimension_semantics=("arbitrary",),
                                             vmem_limit_bytes=VMEM_LIMIT),
        name="dispatch",
    )(seg_len, seg_loc, seg_glob, tile_tot, n_used, grp_expert, x1b, pos_e, gate_e)


def _combine_kernel(len_ref, loc_ref, glob_ref, tot_ref,
                    pos_ref, x1_ref, lng_ref, lnb_ref, y_hbm, o_ref, ybuf_ref, posb_ref, acc_ref, sem):
    i = pl.program_id(0)
    nt = pl.num_programs(0)
    slot = i % 2
    ts = x1_ref.shape[0]

    def copies(s):
        def make(lo, go, n):
            return (pltpu.make_async_copy(y_hbm.at[pl.ds(go, n)], ybuf_ref.at[s, pl.ds(lo, n)], sem.at[s]),)
        return make

    tables = (len_ref, loc_ref, glob_ref)

    @pl.when(i == 0)
    def _():
        ybuf_ref[...] = jnp.zeros(ybuf_ref.shape, ybuf_ref.dtype)
        _segment_copies(0, *tables, copies(0), lambda cp: cp.start())

    @pl.when(i + 1 < nt)
    def _():
        _segment_copies(i + 1, *tables, copies(1 - slot), lambda cp: cp.start())

    for k in range(TOP_K):
        posb_ref[k] = jnp.broadcast_to(pos_ref[:, k:k + 1], (ts, LANES))
    acc_ref[...] = jnp.zeros((ts, D_MODEL), F32)
    _segment_copies(i, *tables, copies(slot), lambda cp: cp.wait())
    lane = lax.broadcasted_iota(jnp.int32, (ts, LANES), 1)

    def gather_rows(c, carry):
        for u in range(SORT_UNROLL):
            r0 = pl.multiple_of((c * SORT_UNROLL + u) * SORT_ROWS, SORT_ROWS)
            halves = []
            for half in range(SORT_ROWS // LANES):
                col = lane + (r0 + half * LANES)
                hit = None
                for k in range(TOP_K):
                    hk = posb_ref[k] == col
                    hit = hk if hit is None else (hit | hk)
                halves.append(jnp.where(hit, 1.0, 0.0).astype(BF16))
            onehot = jnp.concatenate(halves, axis=1)
            acc_ref[...] += jnp.dot(onehot, ybuf_ref[slot, pl.ds(r0, SORT_ROWS), :],
                                    preferred_element_type=F32)
        return carry

    lax.fori_loop(0, _sort_trips(tot_ref[i]), gather_rows, 0)
    o_ref[...] = _layer_norm(DEEPNORM_ALPHA * x1_ref[...] + acc_ref[...], lng_ref[...], lnb_ref[...])


def _combine(seg_len, seg_loc, seg_glob, tile_tot, pos, x1, lng, lnb, y):
    T = x1.shape[0]
    ts = TOKEN_TILE
    grid_spec = pltpu.PrefetchScalarGridSpec(
        num_scalar_prefetch=4,
        grid=(T // ts,),
        in_specs=[pl.BlockSpec((ts, TOP_K), lambda i, *_: (i, 0)),
                  pl.BlockSpec((ts, D_MODEL), lambda i, *_: (i, 0)),
                  pl.BlockSpec((1, D_MODEL), lambda i, *_: (0, 0)),
                  pl.BlockSpec((1, D_MODEL), lambda i, *_: (0, 0)),
                  pl.BlockSpec(memory_space=pl.ANY)],
        out_specs=pl.BlockSpec((ts, D_MODEL), lambda i, *_: (i, 0)),
        scratch_shapes=[pltpu.VMEM((2, SORTED_ROWS, D_MODEL), BF16),
                        pltpu.VMEM((TOP_K, ts, LANES), jnp.int32),
                        pltpu.VMEM((ts, D_MODEL), F32),
                        pltpu.SemaphoreType.DMA((2,))],
    )
    return pl.pallas_call(
        _combine_kernel,
        grid_spec=grid_spec,
        out_shape=jax.ShapeDtypeStruct((T, D_MODEL), F32),
        compiler_params=pltpu.CompilerParams(dimension_semantics=("arbitrary",),
                                             vmem_limit_bytes=VMEM_LIMIT),
        name="combine",
    )(seg_len, seg_loc, seg_glob, tile_tot, pos, x1, lng, lnb, y)


def _moe(layer, x1, x1b, pos_t, pos_e, gate_e, cnt, wup, bup, wdn, bdn, lng, lnb):
    T = x1.shape[0]
    ts = TOKEN_TILE
    nt = T // ts
    r = EXPERT_ROWS
    n_blocks = -(-(T * TOP_K + nt * N_EXPERTS * (SEG_ALIGN - 1) + N_EXPERTS * (r - 1)) // r)
    counts = cnt[:, 0].reshape(nt, N_EXPERTS).astype(jnp.int32)
    seg_len = (counts + SEG_ALIGN - 1) // SEG_ALIGN * SEG_ALIGN
    seg_loc = jnp.cumsum(seg_len, axis=1) - seg_len
    tile_tot = jnp.sum(seg_len, axis=1)
    region = jnp.sum(seg_len, axis=0)
    region = (region + r - 1) // r * r
    region_end = jnp.cumsum(region)
    seg_glob = (region_end - region)[None, :] + jnp.cumsum(seg_len, axis=0) - seg_len
    n_used = (region_end[-1:] // r).astype(jnp.int32)
    block_start = jnp.arange(n_blocks, dtype=jnp.int32) * r
    block_expert = jnp.minimum(jnp.sum(region_end[None, :] <= block_start[:, None], axis=1),
                               N_EXPERTS - 1).astype(jnp.int32)
    used = jnp.sum(seg_len, axis=0)
    zero_row = jnp.zeros((1, N_EXPERTS), jnp.int32)
    tables = (jnp.concatenate([seg_len, (region - used)[None, :]]).reshape(-1),
              jnp.concatenate([seg_loc, zero_row]).reshape(-1),
              jnp.concatenate([seg_glob, (region_end - region + used)[None, :]]).reshape(-1),
              jnp.concatenate([tile_tot, jnp.zeros((1,), jnp.int32)]))
    group_start = jnp.arange(SORT_GROUPS, dtype=jnp.int32) * SEG_ALIGN
    seg_end = seg_loc + seg_len
    grp_expert = jnp.minimum(jnp.sum(seg_end[:, None, :] <= group_start[None, :, None], axis=2), N_EXPERTS - 1)
    grp_expert = jnp.concatenate([grp_expert, jnp.zeros((1, SORT_GROUPS), jnp.int32)]).astype(jnp.int32)
    rows, row_gate = _dispatch(*tables, n_used, grp_expert.reshape(-1), x1b, pos_e, gate_e, n_blocks * r)
    block_rows = jnp.clip((region_end - region + used)[block_expert] - block_start, 0, r).astype(jnp.int32)
    y = _experts(layer, block_expert, n_used, block_rows, rows, row_gate, wup, bup, wdn, bdn)
    return _combine(*tables, pos_t.T, x1, lng, lnb, y)


def kernel(x, mem, positions, a_w_in, a_b_in, a_dw, a_dw_b, a_cn_g, a_cn_b, a_w_out, a_b_out, w_kv_shared, b_w_q, b_w_out, b_b_out, mem_w_kv, ln_g, ln_b, router_w, router_b, exp_w_up, exp_b_up, exp_w_down, exp_b_down):
    B, S, D = x.shape
    T = B * S
    assert D == D_MODEL and S % DIL_TILE == 0 and S % TOKEN_TILE == 0
    xt = x.reshape(T, D)
    kst, vst = _mem_kv(mem.astype(BF16), mem_w_kv.astype(BF16))
    pos = positions.reshape(T, 1).astype(jnp.int32)
    half = ROT_DIM // 2
    inv_freq = jnp.power(ROPE_THETA, -jnp.arange(half, dtype=F32) / half)
    e = jnp.arange(LANES) % HEAD_DIM
    freq = jnp.where(e < ROT_DIM, inv_freq[e % half], 0.0).reshape(1, LANES).astype(F32)
    dw_pad = jnp.pad(a_dw, ((0, 0), (0, 32 - CONV_WIDTH), (0, 0)))
    b_up = exp_b_up.reshape(DEPTH, N_EXPERTS, 1, 2 * D_FF)
    b_down = exp_b_down.reshape(DEPTH, N_EXPERTS, 1, D_MODEL)
    shared_kv = None
    for l in range(DEPTH):
        wrt = router_w[l].T
        br = router_b[l].reshape(N_EXPERTS, 1)
        lng0, lnb0 = _row(ln_g[l, 0]), _row(ln_b[l, 0])
        if l < N_A_LAYERS:
            outs = _layer_a(xt, S, a_w_in[l].astype(BF16), _row(a_b_in[l]), dw_pad[l], _row(a_dw_b[l]),
                            _row(a_cn_g[l]), _row(a_cn_b[l]), kst[l], vst[l],
                            a_w_out[l].astype(BF16), _row(a_b_out[l]), lng0, lnb0, wrt, br)
        else:
            j = l - N_A_LAYERS
            if j == 0:
                q, qm, k, v = _layer_b_proj(xt, B, S, pos, freq, b_w_q[j].astype(BF16), w_kv_shared.astype(BF16))
                shared_kv = (k, v)
            else:
                q, qm = _layer_b_proj(xt, B, S, pos, freq, b_w_q[j].astype(BF16), None)
            dil = []
            for g, (window, dilation) in enumerate(DIL_GROUPS):
                assert window // dilation == DIL_BLOCK and DIL_TILE % (DIL_BLOCK * dilation) == 0
                dil += _dilated_group(g, dilation, q, shared_kv[0], shared_kv[1])
            outs = _layer_b_out(xt, S, dil, qm, kst[l], vst[l], b_w_out[j].astype(BF16), _row(b_b_out[j]),
                                lng0, lnb0, wrt, br)
        x1, x1b, pos_t, pos_e, gate_e, cnt = outs
        xt = _moe(l, x1, x1b, pos_t, pos_e, gate_e, cnt, exp_w_up, b_up, exp_w_down, b_down,
                  _row(ln_g[l, 1]), _row(ln_b[l, 1]))
    return xt.reshape(B, S, D)
```

```python
import functools

import jax
import jax.numpy as jnp
from jax import lax
from jax.experimental import pallas as pl
from jax.experimental.pallas import tpu as pltpu

D_MODEL = 1024
DEPTH = 4
N_A_LAYERS = 2
HEAD_DIM = 64
ROT_DIM = 16
ROPE_THETA = 500000.0
CONV_CH = 768
CONV_WIDTH = 31
MEM_TOKENS = 256
MEM_WIDTH = 256
DIL_GROUPS = ((128, 1), (512, 4), (2048, 16))
DIL_WIDTH = 256
DIL_BLOCK = 128
N_EXPERTS = 32
TOP_K = 4
D_FF = 1024
SWIGLU_ALPHA = 1.702
SWIGLU_LIMIT = 7.0
DEEPNORM_ALPHA = (2 * DEPTH) ** 0.25
LN_EPS = 1e-5
ATTN_SCALE = HEAD_DIM ** -0.5

LANES = 128
TOKEN_TILE = 512
DIL_TILE = 2048
EXPERT_ROWS = 512
CONV_ROWS = 64
CONV_LANES = 384
SEG_ALIGN = 16
SEG_BITS = tuple(SEG_ALIGN << b for b in reversed(range(6)))
SORT_ROWS = 256
SORT_UNROLL = 5
SORTED_ROWS = 2560
SORT_GROUPS = SORTED_ROWS // SEG_ALIGN
VMEM_LIMIT = 56 * 1024 * 1024

F32 = jnp.float32
BF16 = jnp.bfloat16
NEG = -1e30


def _layer_norm(v, g, b):
    mu = jnp.mean(v, axis=-1, keepdims=True)
    c = v - mu
    var = jnp.mean(c * c, axis=-1, keepdims=True)
    return c * lax.rsqrt(var + LN_EPS) * g + b


def _nt_dot(a, b, **kw):
    return lax.dot_general(a, b, (((1,), (1,)), ((), ())), preferred_element_type=F32, **kw)


def _mem_attention(qm_bf16, kst_ref, vst_ref):
    out = None
    for h in range(4):
        s = _nt_dot(qm_bf16, kst_ref[h])
        m = jnp.max(s, axis=-1, keepdims=True)
        p = jnp.exp(s - m)
        l = jnp.sum(p, axis=-1, keepdims=True)
        pn = (p * (1.0 / l)).astype(BF16)
        u = jnp.dot(pn, vst_ref[h], preferred_element_type=F32)
        out = u if out is None else out + u
    return out


def _residual_ln_router(x_res, y, lng_ref, lnb_ref, wrt_ref, br_ref,
                        x1_ref, x1b_ref, pos_ref, pose_ref, gatee_ref, cnt_ref):
    ts = x_res.shape[0]
    x1 = _layer_norm(DEEPNORM_ALPHA * x_res + y, lng_ref[...], lnb_ref[...])
    x1_ref[...] = x1
    x1b_ref[...] = x1.astype(BF16)
    logit = _nt_dot(wrt_ref[...], x1, precision=lax.Precision.HIGHEST) + br_ref[...]
    row = lax.broadcasted_iota(jnp.int32, (N_EXPERTS, ts), 0)
    vals, onehots = [], []
    for k in range(TOP_K):
        m = jnp.max(logit, axis=0, keepdims=True)
        idx = jnp.min(jnp.where(logit == m, row, N_EXPERTS), axis=0, keepdims=True)
        oh = row == idx
        logit = jnp.where(oh, -jnp.inf, logit)
        vals.append(m)
        onehots.append(oh)
    exps = [jnp.exp(v - vals[0]) for v in vals]
    inv = 1.0 / (exps[0] + exps[1] + exps[2] + exps[3])
    gate_e = jnp.zeros((N_EXPERTS, ts), F32)
    for k in range(TOP_K):
        gate_e = jnp.where(onehots[k], exps[k] * inv, gate_e)
    gatee_ref[...] = gate_e
    oh_all = (onehots[0] | onehots[1] | onehots[2] | onehots[3])
    oh_f = jnp.where(oh_all, 1.0, 0.0)
    ri = lax.broadcasted_iota(jnp.int32, (ts, ts), 0)
    ci = lax.broadcasted_iota(jnp.int32, (ts, ts), 1)
    tri = jnp.where(ri < ci, 1.0, 0.0).astype(BF16)
    cum = jnp.dot(oh_f.astype(BF16), tri, preferred_element_type=F32)
    cnt = jnp.sum(oh_f, axis=1, keepdims=True)
    cnt_b = jnp.broadcast_to(cnt, (N_EXPERTS, LANES))
    cnt_ref[...] = cnt_b
    units = jnp.floor((cnt_b + (SEG_ALIGN - 1)) * (1.0 / SEG_ALIGN))
    er = lax.broadcasted_iota(jnp.int32, (N_EXPERTS, N_EXPERTS), 0)
    ec = lax.broadcasted_iota(jnp.int32, (N_EXPERTS, N_EXPERTS), 1)
    low_tri = jnp.where(ec < er, 1.0, 0.0).astype(BF16)
    seg_start = jnp.dot(low_tri, units.astype(BF16), preferred_element_type=F32)[:, 0:1] * SEG_ALIGN
    pos_all = cum + seg_start
    pose_ref[...] = jnp.where(oh_all, pos_all, -1.0).astype(jnp.int32)
    for k in range(TOP_K):
        rk = jnp.sum(jnp.where(onehots[k], pos_all, 0.0), axis=0, keepdims=True)
        pos_ref[k:k + 1, :] = rk.astype(jnp.int32)


def _mem_kv_kernel(mem_ref, w_ref, kst_ref, vst_ref):
    kv = jnp.dot(mem_ref[...], w_ref[...], preferred_element_type=F32)
    k = kv[:, :MEM_WIDTH] * ATTN_SCALE
    v = kv[:, MEM_WIDTH:]
    head = lax.broadcasted_iota(jnp.int32, (MEM_TOKENS, MEM_WIDTH), 1) // HEAD_DIM
    for h in range(4):
        kst_ref[h] = jnp.where(head == h, k, 0.0).astype(BF16)
        vst_ref[h] = jnp.where(head == h, v, 0.0).astype(BF16)


def _mem_kv(mem_b, w_b):
    B = mem_b.shape[0]
    out = jax.ShapeDtypeStruct((DEPTH, B, 4, MEM_TOKENS, MEM_WIDTH), BF16)
    return pl.pallas_call(
        _mem_kv_kernel,
        grid=(DEPTH, B),
        in_specs=[pl.BlockSpec((None, MEM_TOKENS, D_MODEL), lambda l, b: (b, 0, 0)),
                  pl.BlockSpec((None, D_MODEL, 2 * MEM_WIDTH), lambda l, b: (l, 0, 0))],
        out_specs=[pl.BlockSpec((None, None, 4, MEM_TOKENS, MEM_WIDTH), lambda l, b: (l, b, 0, 0, 0))] * 2,
        out_shape=[out, out],
        name="mem_kv",
    )(mem_b, w_b)


def _layer_a_kernel(tiles_per_batch,
                    x_ref, win_ref, bin_ref, dw_ref, dwb_ref, cng_ref, cnb_ref, kst_ref, vst_ref,
                    wout_ref, bout_ref, lng_ref, lnb_ref, wrt_ref, br_ref,
                    x1_ref, x1b_ref, pos_ref, pose_ref, gatee_ref, cnt_ref,
                    hpad_ref, shift_ref, cat_ref):
    ts = x_ref.shape[0]
    i = pl.program_id(0)
    x = x_ref[...]
    h = jnp.dot(x.astype(BF16), win_ref[...], preferred_element_type=F32) + bin_ref[...]
    hg = h[:, :CONV_CH] * jax.nn.sigmoid(h[:, CONV_CH:2 * CONV_CH])
    qm = h[:, 2 * CONV_CH:].astype(BF16)

    @pl.when(i % tiles_per_batch == 0)
    def _():
        hpad_ref[0:32, :] = jnp.zeros((32, CONV_CH), F32)

    hpad_ref[32:32 + ts, :] = hg
    for b in range(8):
        n = ts + 8 * ((CONV_WIDTH - 1 - b) // 8)
        for r0 in range(0, n, 128):
            rows = min(128, n - r0)
            shift_ref[b, r0:r0 + rows, :] = hpad_ref[pl.ds(2 + b + r0, rows), :]
    hpad_ref[0:32, :] = hpad_ref[ts:ts + 32, :]

    for c0 in range(0, CONV_CH, CONV_LANES):
        lanes = slice(c0, c0 + CONV_LANES)

        def conv_rows(c, carry, lanes=lanes):
            r0 = pl.multiple_of(c * CONV_ROWS, CONV_ROWS)
            acc = jnp.zeros((CONV_ROWS, CONV_LANES), F32) + dwb_ref[:, lanes]
            for k in range(CONV_WIDTH):
                a, b = divmod(k, 8)
                acc = acc + shift_ref[b, pl.ds(r0 + 8 * a, CONV_ROWS), lanes] * dw_ref[k:k + 1, lanes]
            hpad_ref[pl.ds(32 + r0, CONV_ROWS), lanes] = acc
            return carry

        lax.fori_loop(0, ts // CONV_ROWS, conv_rows, 0)

    def norm_rows(c, carry):
        r0 = pl.multiple_of(c * CONV_ROWS, CONV_ROWS)
        cn = _layer_norm(hpad_ref[pl.ds(32 + r0, CONV_ROWS), :], cng_ref[...], cnb_ref[...])
        cat_ref[pl.ds(r0, CONV_ROWS), 0:CONV_CH] = (cn * jax.nn.sigmoid(cn)).astype(BF16)
        return carry

    lax.fori_loop(0, ts // CONV_ROWS, norm_rows, 0)

    cat_ref[:, CONV_CH:] = _mem_attention(qm, kst_ref, vst_ref).astype(BF16)
    y = jnp.dot(cat_ref[...], wout_ref[...], preferred_element_type=F32) + bout_ref[...]
    _residual_ln_router(x, y, lng_ref, lnb_ref, wrt_ref, br_ref,
                        x1_ref, x1b_ref, pos_ref, pose_ref, gatee_ref, cnt_ref)


def _row(v):
    return v.reshape(1, -1).astype(F32)


def _const_spec(shape):
    nd = len(shape)
    return pl.BlockSpec(shape, lambda i: (0,) * nd)


def _router_out(T, ts):
    nt = T // ts
    shapes = [jax.ShapeDtypeStruct((T, D_MODEL), F32), jax.ShapeDtypeStruct((T, D_MODEL), BF16),
              jax.ShapeDtypeStruct((TOP_K, T), jnp.int32), jax.ShapeDtypeStruct((N_EXPERTS, T), jnp.int32),
              jax.ShapeDtypeStruct((N_EXPERTS, T), F32), jax.ShapeDtypeStruct((nt * N_EXPERTS, LANES), F32)]
    specs = [pl.BlockSpec((ts, D_MODEL), lambda i: (i, 0)), pl.BlockSpec((ts, D_MODEL), lambda i: (i, 0)),
             pl.BlockSpec((TOP_K, ts), lambda i: (0, i)), pl.BlockSpec((N_EXPERTS, ts), lambda i: (0, i)),
             pl.BlockSpec((N_EXPERTS, ts), lambda i: (0, i)), pl.BlockSpec((N_EXPERTS, LANES), lambda i: (i, 0))]
    return shapes, specs


def _layer_a(x, S, win, bin_, dw, dwb, cng, cnb, kst, vst, wout, bout, lng, lnb, wrt, br):
    T = x.shape[0]
    ts = TOKEN_TILE
    tpb = S // ts
    out_shapes, out_specs = _router_out(T, ts)
    kv_spec = pl.BlockSpec((None, 4, MEM_TOKENS, MEM_WIDTH), lambda i: (i // tpb, 0, 0, 0))
    a_in = 2 * CONV_CH + MEM_WIDTH
    return pl.pallas_call(
        functools.partial(_layer_a_kernel, tpb),
        grid=(T // ts,),
        in_specs=[pl.BlockSpec((ts, D_MODEL), lambda i: (i, 0)),
                  _const_spec((D_MODEL, a_in)), _const_spec((1, a_in)),
                  _const_spec((32, CONV_CH)), _const_spec((1, CONV_CH)),
                  _const_spec((1, CONV_CH)), _const_spec((1, CONV_CH)),
                  kv_spec, kv_spec,
                  _const_spec((D_MODEL, D_MODEL)), _const_spec((1, D_MODEL)),
                  _const_spec((1, D_MODEL)), _const_spec((1, D_MODEL)),
                  _const_spec((N_EXPERTS, D_MODEL)), _const_spec((N_EXPERTS, 1))],
        out_specs=out_specs,
        out_shape=out_shapes,
        scratch_shapes=[pltpu.VMEM((ts + 32, CONV_CH), F32),
                        pltpu.VMEM((8, ts + 24, CONV_CH), F32),
                        pltpu.VMEM((ts, D_MODEL), BF16)],
        compiler_params=pltpu.CompilerParams(dimension_semantics=("arbitrary",),
                                             vmem_limit_bytes=VMEM_LIMIT),
        name="layer_a",
    )(x, win, bin_, dw, dwb, cng, cnb, kst, vst, wout, bout, lng, lnb, wrt, br)


def _rotary(v, cos_t, sin_t, low):
    partner = jnp.where(low, pltpu.roll(v, LANES - 8, axis=1), pltpu.roll(v, 8, axis=1))
    return v * cos_t + partner * sin_t


def _rope_tables(pos_ref, freq_ref):
    e = lax.broadcasted_iota(jnp.int32, (1, LANES), 1) % HEAD_DIM
    ang = pos_ref[...].astype(F32) * freq_ref[...]
    low = e < ROT_DIM // 2
    rot = e < ROT_DIM
    cos_t = jnp.where(rot, jnp.cos(ang), 1.0)
    sn = jnp.sin(ang)
    sin_t = jnp.where(low, -sn, jnp.where(rot, sn, 0.0))
    return cos_t, sin_t, low


def _layer_b_proj_kernel(with_kv, x_ref, pos_ref, freq_ref, wq_ref, *rest):
    if with_kv:
        wkv_ref, q_ref, qm_ref, k_ref, v_ref = rest
    else:
        q_ref, qm_ref = rest
    xb = x_ref[...].astype(BF16)
    cos_t, sin_t, low = _rope_tables(pos_ref, freq_ref)
    q = jnp.dot(xb, wq_ref[...], preferred_element_type=F32)
    for g in range(3):
        for s in range(2):
            c0 = g * DIL_WIDTH + s * LANES
            q_ref[g, s] = _rotary(q[:, c0:c0 + LANES], cos_t, sin_t, low) * ATTN_SCALE
    qm_ref[...] = q[:, 3 * DIL_WIDTH:].astype(BF16)
    if with_kv:
        kv = jnp.dot(xb, wkv_ref[...], preferred_element_type=F32)
        for g in range(3):
            for s in range(2):
                c0 = g * 2 * DIL_WIDTH + s * LANES
                k_ref[g, s] = _rotary(kv[:, c0:c0 + LANES], cos_t, sin_t, low)
                v_ref[g, s] = kv[:, c0 + DIL_WIDTH:c0 + DIL_WIDTH + LANES]


def _layer_b_proj(x, B, S, pos, freq, wq, wkv):
    T = x.shape[0]
    ts = TOKEN_TILE
    tpb = S // ts
    with_kv = wkv is not None
    slab = jax.ShapeDtypeStruct((3, B, 2, S, LANES), F32)
    slab_spec = pl.BlockSpec((3, None, 2, ts, LANES), lambda i: (0, i // tpb, 0, i % tpb, 0))
    in_specs = [pl.BlockSpec((ts, D_MODEL), lambda i: (i, 0)),
                pl.BlockSpec((ts, 1), lambda i: (i, 0)),
                _const_spec((1, LANES)),
                _const_spec((D_MODEL, D_MODEL))]
    args = [x, pos, freq, wq]
    out_shape = [slab, jax.ShapeDtypeStruct((T, MEM_WIDTH), BF16)]
    out_specs = [slab_spec, pl.BlockSpec((ts, MEM_WIDTH), lambda i: (i, 0))]
    if with_kv:
        in_specs.append(_const_spec((D_MODEL, 6 * DIL_WIDTH)))
        args.append(wkv)
        out_shape += [slab, slab]
        out_specs += [slab_spec, slab_spec]
    return pl.pallas_call(
        functools.partial(_layer_b_proj_kernel, with_kv),
        grid=(T // ts,),
        in_specs=in_specs, out_specs=out_specs, out_shape=out_shape,
        compiler_params=pltpu.CompilerParams(dimension_semantics=("arbitrary",),
                                             vmem_limit_bytes=VMEM_LIMIT),
        name="layer_b_proj_kv" if with_kv else "layer_b_proj",
    )(*args)


def _dilated_kernel(d, q_ref, k_ref, v_ref, o_ref, lse_ref, kbuf_ref, vbuf_ref):
    i = pl.program_id(1)
    tile = DIL_TILE
    blk = DIL_BLOCK

    @pl.when(i == 0)
    def _():
        kbuf_ref[:, 0:tile, :] = jnp.zeros((2, tile, LANES), F32)
        vbuf_ref[:, 0:tile, :] = jnp.zeros((2, tile, LANES), F32)

    kbuf_ref[:, tile:, :] = k_ref[...]
    vbuf_ref[:, tile:, :] = v_ref[...]

    def rows(start, n):
        return pl.ds(start, n) if d == 1 else pl.ds(start, n, stride=d)

    qi = lax.broadcasted_iota(jnp.int32, (blk, 2 * blk), 0)
    ki = lax.broadcasted_iota(jnp.int32, (blk, 2 * blk), 1)
    dist = qi + blk - ki
    band = (dist >= 0) & (dist <= blk)
    head = lax.broadcasted_iota(jnp.int32, (1, LANES), 1) // HEAD_DIM

    def block(j, carry):
        c = j // d
        r = j % d
        qs = c * (blk * d) + r
        ks = tile + qs - blk * d
        first = jnp.logical_and(i == 0, c == 0)
        valid = band & jnp.logical_or(jnp.logical_not(first), ki >= blk)
        for s in range(2):
            q = q_ref[s, rows(qs, blk), :].astype(BF16)
            kk = kbuf_ref[s, rows(ks, 2 * blk), :]
            vv = vbuf_ref[s, rows(ks, 2 * blk), :]
            o_acc = jnp.zeros((blk, LANES), F32)
            lse_acc = jnp.zeros((blk, LANES), F32)
            for h in range(2):
                hm = head == h
                sc = _nt_dot(q, jnp.where(hm, kk, 0.0).astype(BF16))
                sc = jnp.where(valid, sc, NEG)
                m = jnp.max(sc, axis=-1, keepdims=True)
                p = jnp.exp(sc - m)
                l = jnp.sum(p, axis=-1, keepdims=True)
                u = jnp.dot(p.astype(BF16), jnp.where(hm, vv, 0.0).astype(BF16),
                            preferred_element_type=F32)
                o_acc = o_acc + u * (1.0 / l)
                lse_acc = jnp.where(hm, m + jnp.log(l), lse_acc)
            o_ref[s, rows(qs, blk), :] = o_acc
            lse_ref[s, rows(qs, blk), :] = lse_acc
        return carry

    lax.fori_loop(0, tile // blk, block, 0, unroll=4)
    kbuf_ref[:, 0:tile, :] = kbuf_ref[:, tile:, :]
    vbuf_ref[:, 0:tile, :] = vbuf_ref[:, tile:, :]


def _dilated_group(g, d, q, k, v):
    _, B, _, S, _ = q.shape
    tile = DIL_TILE
    in_spec = pl.BlockSpec((None, None, 2, tile, LANES), lambda b, i: (g, b, 0, i, 0))
    out_spec = pl.BlockSpec((None, 2, tile, LANES), lambda b, i: (b, 0, i, 0))
    out = jax.ShapeDtypeStruct((B, 2, S, LANES), F32)
    return pl.pallas_call(
        functools.partial(_dilated_kernel, d),
        grid=(B, S // tile),
        in_specs=[in_spec, in_spec, in_spec],
        out_specs=[out_spec, out_spec],
        out_shape=[out, out],
        scratch_shapes=[pltpu.VMEM((2, 2 * tile, LANES), F32), pltpu.VMEM((2, 2 * tile, LANES), F32)],
        compiler_params=pltpu.CompilerParams(dimension_semantics=("arbitrary", "arbitrary"),
                                             vmem_limit_bytes=VMEM_LIMIT),
        name=f"dilated_d{d}",
    )(q, k, v)


def _layer_b_out_kernel(x_ref, o0_ref, l0_ref, o1_ref, l1_ref, o2_ref, l2_ref, qm_ref, kst_ref, vst_ref,
                        wout_ref, bout_ref, lng_ref, lnb_ref, wrt_ref, br_ref,
                        x1_ref, x1b_ref, pos_ref, pose_ref, gatee_ref, cnt_ref, cat_ref):
    for s in range(2):
        lses = [l0_ref[s], l1_ref[s], l2_ref[s]]
        outs = [o0_ref[s], o1_ref[s], o2_ref[s]]
        m = jnp.maximum(jnp.maximum(lses[0], lses[1]), lses[2])
        es = [jnp.exp(l - m) for l in lses]
        inv = 1.0 / (es[0] + es[1] + es[2])
        dil = (es[0] * inv) * outs[0] + (es[1] * inv) * outs[1] + (es[2] * inv) * outs[2]
        cat_ref[:, s * LANES:(s + 1) * LANES] = dil.astype(BF16)
    cat_ref[:, DIL_WIDTH:] = _mem_attention(qm_ref[...], kst_ref, vst_ref).astype(BF16)
    y = jnp.dot(cat_ref[...], wout_ref[...], preferred_element_type=F32) + bout_ref[...]
    _residual_ln_router(x_ref[...], y, lng_ref, lnb_ref, wrt_ref, br_ref,
                        x1_ref, x1b_ref, pos_ref, pose_ref, gatee_ref, cnt_ref)


def _layer_b_out(x, S, dil, qm, kst, vst, wout, bout, lng, lnb, wrt, br):
    T = x.shape[0]
    ts = TOKEN_TILE
    tpb = S // ts
    out_shapes, out_specs = _router_out(T, ts)
    kv_spec = pl.BlockSpec((None, 4, MEM_TOKENS, MEM_WIDTH), lambda i: (i // tpb, 0, 0, 0))
    slab_spec = pl.BlockSpec((None, 2, ts, LANES), lambda i: (i // tpb, 0, i % tpb, 0))
    b_out = DIL_WIDTH + MEM_WIDTH
    return pl.pallas_call(
        _layer_b_out_kernel,
        grid=(T // ts,),
        in_specs=[pl.BlockSpec((ts, D_MODEL), lambda i: (i, 0))] + [slab_spec] * 6 + [
            pl.BlockSpec((ts, MEM_WIDTH), lambda i: (i, 0)), kv_spec, kv_spec,
            _const_spec((b_out, D_MODEL)), _const_spec((1, D_MODEL)),
            _const_spec((1, D_MODEL)), _const_spec((1, D_MODEL)),
            _const_spec((N_EXPERTS, D_MODEL)), _const_spec((N_EXPERTS, 1))],
        out_specs=out_specs,
        out_shape=out_shapes,
        scratch_shapes=[pltpu.VMEM((ts, b_out), BF16)],
        compiler_params=pltpu.CompilerParams(dimension_semantics=("arbitrary",),
                                             vmem_limit_bytes=VMEM_LIMIT),
        name="layer_b_out",
    )(x, *dil, qm, kst, vst, wout, bout, lng, lnb, wrt, br)


def _used_block(i, nu):
    return jnp.maximum(jnp.minimum(i, nu[0] - 1), 0)


def _expert_kernel(be_ref, nused_ref, valid_ref, rows_ref, gate_ref, wup_ref, bup_ref, wdn_ref, bdn_ref, y_ref,
                   wup_bf_ref, wdn_bf_ref):
    i = pl.program_id(0)
    used = i < nused_ref[0]
    half = rows_ref.shape[0] // 2

    @pl.when(jnp.logical_not(used))
    def _():
        y_ref[...] = jnp.zeros(y_ref.shape, y_ref.dtype)

    b = _used_block(i, nused_ref)
    new_expert = jnp.logical_or(i == 0, be_ref[b] != be_ref[jnp.maximum(b - 1, 0)])

    @pl.when(jnp.logical_and(used, new_expert))
    def _():
        wup_bf_ref[...] = wup_ref[...].astype(BF16)
        wdn_bf_ref[...] = wdn_ref[...].astype(BF16)

    def ffn(rs):
        h = jnp.dot(rows_ref[rs, :], wup_bf_ref[...], preferred_element_type=F32) + bup_ref[...]
        g = jnp.minimum(h[:, :D_FF], SWIGLU_LIMIT)
        lin = jnp.clip(h[:, D_FF:], -SWIGLU_LIMIT, SWIGLU_LIMIT)
        act = g * jax.nn.sigmoid(SWIGLU_ALPHA * g) * (lin + 1.0)
        y = jnp.dot(act.astype(BF16), wdn_bf_ref[...], preferred_element_type=F32) + bdn_ref[...]
        gate = gate_ref[rs, :]
        for c in range(D_MODEL // LANES):
            y_ref[rs, c * LANES:(c + 1) * LANES] = (
                y[:, c * LANES:(c + 1) * LANES] * gate).astype(y_ref.dtype)

    data_rows = valid_ref[b]

    @pl.when(jnp.logical_and(used, data_rows > half))
    def _():
        ffn(slice(None))

    @pl.when(jnp.logical_and(used, data_rows <= half))
    def _():
        ffn(slice(0, half))
        y_ref[half:, :] = jnp.zeros((y_ref.shape[0] - half, D_MODEL), y_ref.dtype)


def _experts(layer, block_expert, n_used, block_rows, rows, row_gate, wup, bup, wdn, bdn):
    nr = rows.shape[0]
    r = EXPERT_ROWS

    def row_map(i, be, nu, br):
        return (_used_block(i, nu), 0)

    def wmap(i, be, nu, br):
        return (layer, be[_used_block(i, nu)], 0, 0)

    grid_spec = pltpu.PrefetchScalarGridSpec(
        num_scalar_prefetch=3,
        grid=(nr // r,),
        in_specs=[pl.BlockSpec((r, D_MODEL), row_map),
                  pl.BlockSpec((r, LANES), row_map),
                  pl.BlockSpec((None, None, D_MODEL, 2 * D_FF), wmap),
                  pl.BlockSpec((None, None, 1, 2 * D_FF), wmap),
                  pl.BlockSpec((None, None, D_FF, D_MODEL), wmap),
                  pl.BlockSpec((None, None, 1, D_MODEL), wmap)],
        out_specs=pl.BlockSpec((r, D_MODEL), lambda i, be, nu, br: (i, 0)),
        scratch_shapes=[pltpu.VMEM((D_MODEL, 2 * D_FF), BF16), pltpu.VMEM((D_FF, D_MODEL), BF16)],
    )
    return pl.pallas_call(
        _expert_kernel,
        grid_spec=grid_spec,
        out_shape=jax.ShapeDtypeStruct((nr, D_MODEL), BF16),
        compiler_params=pltpu.CompilerParams(dimension_semantics=("arbitrary",),
                                             vmem_limit_bytes=VMEM_LIMIT),
        name="experts",
    )(block_expert, n_used, block_rows, rows, row_gate, wup, bup, wdn, bdn)


def _sort_trips(n_rows):
    step = SORT_ROWS * SORT_UNROLL
    return (n_rows + step - 1) // step


def _segment_copies(tile, len_ref, loc_ref, glob_ref, make_copies, act):
    def per_expert(e, carry):
        n = len_ref[tile * N_EXPERTS + e]
        loc = loc_ref[tile * N_EXPERTS + e]
        glob = glob_ref[tile * N_EXPERTS + e]
        off = jnp.int32(0)
        for b, bit in enumerate(SEG_BITS):
            part = n & bit

            @pl.when(part != 0)
            def _():
                lo = pl.multiple_of(loc + off, SEG_ALIGN)
                go = pl.multiple_of(glob + off, SEG_ALIGN)
                for cp in make_copies(lo, go, bit):
                    act(cp, b % 2)

            off = off + part
        return carry

    lax.fori_loop(0, N_EXPERTS, per_expert, 0)


def _dispatch_kernel(len_ref, loc_ref, glob_ref, tot_ref, nused_ref, grp_ref,
                     xb_ref, pose_ref, gatee_ref,
                     rows_hbm, gates_hbm, xs_ref, gs_ref, sel_ref, sem):
    i = pl.program_id(0)
    nt = pl.num_programs(0)
    slot = i % 2
    ts = xb_ref.shape[0]

    def copies(s):
        def make(lo, go, n):
            return (pltpu.make_async_copy(xs_ref.at[s, pl.ds(lo, n)], rows_hbm.at[pl.ds(go, n)], sem.at[0, s]),
                    pltpu.make_async_copy(gs_ref.at[s, pl.ds(lo, n)], gates_hbm.at[pl.ds(go, n)], sem.at[1, s]))
        return make

    tables = (len_ref, loc_ref, glob_ref)

    @pl.when(i >= 2)
    def _():
        _segment_copies(i - 2, *tables, copies(slot), lambda cp, p=0: cp.wait())

    xb = xb_ref[...]
    prow = lax.broadcasted_iota(jnp.int32, (SEG_ALIGN, ts), 0)

    def sort_rows(c, carry):
        for u in range(SORT_UNROLL):
            chunk = c * SORT_UNROLL + u
            r0 = pl.multiple_of(chunk * SORT_ROWS, SORT_ROWS)
            for g in range(SORT_ROWS // SEG_ALIGN):
                e = grp_ref[i * SORT_GROUPS + chunk * (SORT_ROWS // SEG_ALIGN) + g]
                p0 = r0 + g * SEG_ALIGN
                hit = (prow + p0) == pose_ref[pl.ds(e, 1), :]
                sel_ref[u, g * SEG_ALIGN:(g + 1) * SEG_ALIGN, :] = jnp.where(hit, 1.0, 0.0).astype(BF16)
                gsum = jnp.sum(jnp.where(hit, gatee_ref[pl.ds(e, 1), :], 0.0), axis=1, keepdims=True)
                gs_ref[slot, pl.ds(p0, SEG_ALIGN), :] = jnp.broadcast_to(gsum, (SEG_ALIGN, LANES))
            xs_ref[slot, pl.ds(r0, SORT_ROWS), :] = jnp.dot(
                sel_ref[u], xb, preferred_element_type=F32).astype(BF16)
        return carry

    lax.fori_loop(0, _sort_trips(tot_ref[i]), sort_rows, 0)

    @pl.when(i == nt - 1)
    def _():
        xs_ref[slot, 0:EXPERT_ROWS, :] = jnp.zeros((EXPERT_ROWS, D_MODEL), BF16)
        gs_ref[slot, 0:EXPERT_ROWS, :] = jnp.zeros((EXPERT_ROWS, LANES), F32)

    _segment_copies(i, *tables, copies(slot), lambda cp, p=0: cp.start(priority=p))

    def unused_blocks(act):
        def per_block(j, carry):
            for cp in copies(slot)(0, pl.multiple_of(j * EXPERT_ROWS, EXPERT_ROWS), EXPERT_ROWS):
                act(cp)
            return carry
        lax.fori_loop(nused_ref[0], rows_hbm.shape[0] // EXPERT_ROWS, per_block, 0)

    @pl.when(i == nt - 1)
    def _():
        unused_blocks(lambda cp, p=0: cp.start(priority=p))

        @pl.when(i >= 1)
        def _():
            _segment_copies(i - 1, *tables, copies(1 - slot), lambda cp, p=0: cp.wait())
        _segment_copies(i, *tables, copies(slot), lambda cp, p=0: cp.wait())
        unused_blocks(lambda cp, p=0: cp.wait())


def _dispatch(seg_len, seg_loc, seg_glob, tile_tot, n_used, grp_expert, x1b, pos_e, gate_e, n_rows):
    T = x1b.shape[0]
    ts = TOKEN_TILE
    nt = T // ts
    any_spec = pl.BlockSpec(memory_space=pl.ANY)

    def tile(i, *_):
        return jnp.minimum(i, nt - 1)

    grid_spec = pltpu.PrefetchScalarGridSpec(
        num_scalar_prefetch=6,
        grid=(nt + 1,),
        in_specs=[pl.BlockSpec((ts, D_MODEL), lambda i, *_: (tile(i), 0)),
                  pl.BlockSpec((N_EXPERTS, ts), lambda i, *_: (0, tile(i))),
                  pl.BlockSpec((N_EXPERTS, ts), lambda i, *_: (0, tile(i)))],
        out_specs=[any_spec, any_spec],
        scratch_shapes=[pltpu.VMEM((2, SORTED_ROWS, D_MODEL), BF16),
                        pltpu.VMEM((2, SORTED_ROWS, LANES), F32),
                        pltpu.VMEM((SORT_UNROLL, SORT_ROWS, ts), BF16),
                        pltpu.SemaphoreType.DMA((2, 2))],
    )
    return pl.pallas_call(
        _dispatch_kernel,
        grid_spec=grid_spec,
        out_shape=[jax.ShapeDtypeStruct((n_rows, D_MODEL), BF16), jax.ShapeDtypeStruct((n_rows, LANES), F32)],
        compiler_params=pltpu.CompilerParams(dimension_semantics=("arbitrary",),
                                             vmem_limit_bytes=VMEM_LIMIT),
        name="dispatch",
    )(seg_len, seg_loc, seg_glob, tile_tot, n_used, grp_expert, x1b, pos_e, gate_e)


def _combine_kernel(len_ref, loc_ref, glob_ref, tot_ref,
                    pos_ref, x1_ref, lng_ref, lnb_ref, y_hbm, o_ref, ybuf_ref, posb_ref, acc_ref, sem):
    i = pl.program_id(0)
    nt = pl.num_programs(0)
    slot = i % 2
    ts = x1_ref.shape[0]

    def copies(s):
        def make(lo, go, n):
            return (pltpu.make_async_copy(y_hbm.at[pl.ds(go, n)], ybuf_ref.at[s, pl.ds(lo, n)], sem.at[s]),)
        return make

    tables = (len_ref, loc_ref, glob_ref)

    @pl.when(i == 0)
    def _():
        ybuf_ref[...] = jnp.zeros(ybuf_ref.shape, ybuf_ref.dtype)
        _segment_copies(0, *tables, copies(0), lambda cp, p=0: cp.start(priority=p))

    @pl.when(i + 1 < nt)
    def _():
        _segment_copies(i + 1, *tables, copies(1 - slot), lambda cp, p=0: cp.start(priority=p))

    for k in range(TOP_K):
        posb_ref[k] = jnp.broadcast_to(pos_ref[:, k:k + 1], (ts, LANES))
    acc_ref[...] = jnp.zeros((ts, D_MODEL), F32)
    _segment_copies(i, *tables, copies(slot), lambda cp, p=0: cp.wait())
    lane = lax.broadcasted_iota(jnp.int32, (ts, LANES), 1)

    def gather_rows(c, carry):
        for u in range(SORT_UNROLL):
            r0 = pl.multiple_of((c * SORT_UNROLL + u) * SORT_ROWS, SORT_ROWS)
            halves = []
            for half in range(SORT_ROWS // LANES):
                col = lane + (r0 + half * LANES)
                hit = None
                for k in range(TOP_K):
                    hk = posb_ref[k] == col
                    hit = hk if hit is None else (hit | hk)
                halves.append(jnp.where(hit, 1.0, 0.0).astype(BF16))
            onehot = jnp.concatenate(halves, axis=1)
            acc_ref[...] += jnp.dot(onehot, ybuf_ref[slot, pl.ds(r0, SORT_ROWS), :],
                                    preferred_element_type=F32)
        return carry

    lax.fori_loop(0, _sort_trips(tot_ref[i]), gather_rows, 0)
    o_ref[...] = _layer_norm(DEEPNORM_ALPHA * x1_ref[...] + acc_ref[...], lng_ref[...], lnb_ref[...])


def _combine(seg_len, seg_loc, seg_glob, tile_tot, pos, x1, lng, lnb, y):
    T = x1.shape[0]
    ts = TOKEN_TILE
    grid_spec = pltpu.PrefetchScalarGridSpec(
        num_scalar_prefetch=4,
        grid=(T // ts,),
        in_specs=[pl.BlockSpec((ts, TOP_K), lambda i, *_: (i, 0)),
                  pl.BlockSpec((ts, D_MODEL), lambda i, *_: (i, 0)),
                  pl.BlockSpec((1, D_MODEL), lambda i, *_: (0, 0)),
                  pl.BlockSpec((1, D_MODEL), lambda i, *_: (0, 0)),
                  pl.BlockSpec(memory_space=pl.ANY)],
        out_specs=pl.BlockSpec((ts, D_MODEL), lambda i, *_: (i, 0)),
        scratch_shapes=[pltpu.VMEM((2, SORTED_ROWS, D_MODEL), BF16),
                        pltpu.VMEM((TOP_K, ts, LANES), jnp.int32),
                        pltpu.VMEM((ts, D_MODEL), F32),
                        pltpu.SemaphoreType.DMA((2,))],
    )
    return pl.pallas_call(
        _combine_kernel,
        grid_spec=grid_spec,
        out_shape=jax.ShapeDtypeStruct((T, D_MODEL), F32),
        compiler_params=pltpu.CompilerParams(dimension_semantics=("arbitrary",),
                                             vmem_limit_bytes=VMEM_LIMIT),
        name="combine",
    )(seg_len, seg_loc, seg_glob, tile_tot, pos, x1, lng, lnb, y)


def _moe(layer, x1, x1b, pos_t, pos_e, gate_e, cnt, wup, bup, wdn, bdn, lng, lnb):
    T = x1.shape[0]
    ts = TOKEN_TILE
    nt = T // ts
    r = EXPERT_ROWS
    n_blocks = -(-(T * TOP_K + nt * N_EXPERTS * (SEG_ALIGN - 1) + N_EXPERTS * (r - 1)) // r)
    counts = cnt[:, 0].reshape(nt, N_EXPERTS).astype(jnp.int32)
    seg_len = (counts + SEG_ALIGN - 1) // SEG_ALIGN * SEG_ALIGN
    seg_loc = jnp.cumsum(seg_len, axis=1) - seg_len
    tile_tot = jnp.sum(seg_len, axis=1)
    region = jnp.sum(seg_len, axis=0)
    region = (region + r - 1) // r * r
    region_end = jnp.cumsum(region)
    seg_glob = (region_end - region)[None, :] + jnp.cumsum(seg_len, axis=0) - seg_len
    n_used = (region_end[-1:] // r).astype(jnp.int32)
    block_start = jnp.arange(n_blocks, dtype=jnp.int32) * r
    block_expert = jnp.minimum(jnp.sum(region_end[None, :] <= block_start[:, None], axis=1),
                               N_EXPERTS - 1).astype(jnp.int32)
    used = jnp.sum(seg_len, axis=0)
    zero_row = jnp.zeros((1, N_EXPERTS), jnp.int32)
    tables = (jnp.concatenate([seg_len, (region - used)[None, :]]).reshape(-1),
              jnp.concatenate([seg_loc, zero_row]).reshape(-1),
              jnp.concatenate([seg_glob, (region_end - region + used)[None, :]]).reshape(-1),
              jnp.concatenate([tile_tot, jnp.zeros((1,), jnp.int32)]))
    group_start = jnp.arange(SORT_GROUPS, dtype=jnp.int32) * SEG_ALIGN
    seg_end = seg_loc + seg_len
    grp_expert = jnp.minimum(jnp.sum(seg_end[:, None, :] <= group_start[None, :, None], axis=2), N_EXPERTS - 1)
    grp_expert = jnp.concatenate([grp_expert, jnp.zeros((1, SORT_GROUPS), jnp.int32)]).astype(jnp.int32)
    rows, row_gate = _dispatch(*tables, n_used, grp_expert.reshape(-1), x1b, pos_e, gate_e, n_blocks * r)
    block_rows = jnp.clip((region_end - region + used)[block_expert] - block_start, 0, r).astype(jnp.int32)
    y = _experts(layer, block_expert, n_used, block_rows, rows, row_gate, wup, bup, wdn, bdn)
    return _combine(*tables, pos_t.T, x1, lng, lnb, y)


def kernel(x, mem, positions, a_w_in, a_b_in, a_dw, a_dw_b, a_cn_g, a_cn_b, a_w_out, a_b_out, w_kv_shared, b_w_q, b_w_out, b_b_out, mem_w_kv, ln_g, ln_b, router_w, router_b, exp_w_up, exp_b_up, exp_w_down, exp_b_down):
    B, S, D = x.shape
    T = B * S
    assert D == D_MODEL and S % DIL_TILE == 0 and S % TOKEN_TILE == 0
    xt = x.reshape(T, D)
    kst, vst = _mem_kv(mem.astype(BF16), mem_w_kv.astype(BF16))
    pos = positions.reshape(T, 1).astype(jnp.int32)
    half = ROT_DIM // 2
    inv_freq = jnp.power(ROPE_THETA, -jnp.arange(half, dtype=F32) / half)
    e = jnp.arange(LANES) % HEAD_DIM
    freq = jnp.where(e < ROT_DIM, inv_freq[e % half], 0.0).reshape(1, LANES).astype(F32)
    dw_pad = jnp.pad(a_dw, ((0, 0), (0, 32 - CONV_WIDTH), (0, 0)))
    b_up = exp_b_up.reshape(DEPTH, N_EXPERTS, 1, 2 * D_FF)
    b_down = exp_b_down.reshape(DEPTH, N_EXPERTS, 1, D_MODEL)
    shared_kv = None
    for l in range(DEPTH):
        wrt = router_w[l].T
        br = router_b[l].reshape(N_EXPERTS, 1)
        lng0, lnb0 = _row(ln_g[l, 0]), _row(ln_b[l, 0])
        if l < N_A_LAYERS:
            outs = _layer_a(xt, S, a_w_in[l].astype(BF16), _row(a_b_in[l]), dw_pad[l], _row(a_dw_b[l]),
                            _row(a_cn_g[l]), _row(a_cn_b[l]), kst[l], vst[l],
                            a_w_out[l].astype(BF16), _row(a_b_out[l]), lng0, lnb0, wrt, br)
        else:
            j = l - N_A_LAYERS
            if j == 0:
                q, qm, k, v = _layer_b_proj(xt, B, S, pos, freq, b_w_q[j].astype(BF16), w_kv_shared.astype(BF16))
                shared_kv = (k, v)
            else:
                q, qm = _layer_b_proj(xt, B, S, pos, freq, b_w_q[j].astype(BF16), None)
            dil = []
            for g, (window, dilation) in enumerate(DIL_GROUPS):
                assert window // dilation == DIL_BLOCK and DIL_TILE % (DIL_BLOCK * dilation) == 0
                dil += _dilated_group(g, dilation, q, shared_kv[0], shared_kv[1])
            outs = _layer_b_out(xt, S, dil, qm, kst[l], vst[l], b_w_out[j].astype(BF16), _row(b_b_out[j]),
                                lng0, lnb0, wrt, br)
        x1, x1b, pos_t, pos_e, gate_e, cnt = outs
        xt = _moe(l, x1, x1b, pos_t, pos_e, gate_e, cnt, exp_w_up, b_up, exp_w_down, b_down,
                  _row(ln_g[l, 1]), _row(ln_b[l, 1]))
    return xt.reshape(B, S, D)
```
